```python
import math
import jax
import jax.numpy as jnp
from jax import lax
import numpy as np

D_MODEL = 2048
BATCH = 8
SEQ = 2048
DEPTH = 2
DEC_BATCH = 32
DEC_SEQ = 1
PAST_LEN = 8192
PAGE_SIZE = 128

GLA_HEADS = 4
GLA_DK = D_MODEL // 2 // GLA_HEADS
GLA_DV = D_MODEL // GLA_HEADS
GLA_RANK = 16
GLA_TAU = 16.0
GLA_CHUNK = 64
GLA_DK_TOT = GLA_HEADS * GLA_DK
GLA_DV_TOT = GLA_HEADS * GLA_DV
GLA_IN = 2 * GLA_DK_TOT + GLA_DV_TOT + GLA_RANK + GLA_DV_TOT
DSW_GROUPS = ((128, 1), (512, 4), (2048, 16))
N_GROUPS = len(DSW_GROUPS)
DSW_HEADS = 16
DSW_HD = D_MODEL // DSW_HEADS
DSW_WINDOW_MAX = max(w for w, _ in DSW_GROUPS)
N_BUCKETS = 32
BUCKET_MAX_DIST = DSW_WINDOW_MAX
D_FF_DENSE = 256 * ((8 * D_MODEL // 3 + 255) // 256)
N_EXPERTS = 8
TOP_K = 2
D_FF_EXPERT = 7 * D_MODEL // 2
ALPHA = (2 * DEPTH) ** 0.25
BETA = (8 * DEPTH) ** -0.25
LN_EPS = 1e-5
NEG_INF = -1e30

kernel_name = 'yoco_gla_dilated_swa_decoder_step'


def _layer_norm(x, g, b):
    xf = x.astype(jnp.float32)
    mu = jnp.mean(xf, axis=-1, keepdims=True)
    var = jnp.mean(jnp.square(xf - mu), axis=-1, keepdims=True)
    return ((xf - mu) * lax.rsqrt(var + LN_EPS) * g.astype(jnp.float32) + b.astype(jnp.float32)).astype(x.dtype)


def _swiglu(x, w_gu, w_down):
    g, u = jnp.split(x @ w_gu, 2, axis=-1)
    return (jax.nn.silu(g) * u) @ w_down


def _moe(x, w_router, w_gu, w_down):
    B, T, D = x.shape
    xt = x.reshape(B * T, D)
    logits = (xt @ w_router).astype(jnp.float32)
    top_v, top_i = lax.top_k(logits, TOP_K)
    top_w = jax.nn.softmax(top_v, axis=-1)
    gates = jnp.sum(jax.nn.one_hot(top_i, N_EXPERTS, dtype=jnp.float32) * top_w[..., None], axis=1)
    y = jnp.zeros((B * T, D), jnp.float32)
    for e in range(N_EXPERTS):
        y = y + gates[:, e:e + 1] * _swiglu(xt, w_gu[e], w_down[e]).astype(jnp.float32)
    return y.reshape(B, T, D).astype(x.dtype)


def _gla_recurrence(q, k, v, log_a, s0):
    B, T, H, dk = q.shape
    dv = v.shape[-1]
    C = GLA_CHUNK if T % GLA_CHUNK == 0 else T
    n = T // C

    def blocks(x):
        return x.astype(jnp.float32).reshape(B, n, C, H, x.shape[-1]).transpose(1, 0, 3, 2, 4)

    causal = jnp.tril(jnp.ones((C, C), dtype=bool))

    def step(S, inp):
        qc, kc, vc, ac = inp
        b = jnp.cumsum(ac, axis=2)
        b_last = b[:, :, -1:, :]
        q_dec = qc * jnp.exp(b)
        k_dec = kc * jnp.exp(-b)
        A = jnp.where(causal, jnp.einsum('bhid,bhjd->bhij', q_dec, k_dec), 0.0)
        o = jnp.einsum('bhid,bhde->bhie', q_dec, S) + jnp.einsum('bhij,bhje->bhie', A, vc)
        S = S * jnp.exp(b_last[:, :, 0, :, None]) + jnp.einsum('bhjd,bhje->bhde', kc * jnp.exp(b_last - b), vc)
        return S, o

    S, o = lax.scan(step, s0.astype(jnp.float32), (blocks(q), blocks(k), blocks(v), blocks(log_a)))
    o = o.transpose(1, 0, 3, 2, 4).reshape(B, T, H, dv)
    return o, S


def _gla_mixer(x, s0, w_in, w_a2, b_a, norm_g, w_out):
    B, T, _ = x.shape
    proj = x @ w_in
    q, k, v, a_lr, r = jnp.split(proj, [GLA_DK_TOT, 2 * GLA_DK_TOT, 2 * GLA_DK_TOT + GLA_DV_TOT,
                                        2 * GLA_DK_TOT + GLA_DV_TOT + GLA_RANK], axis=-1)
    q = q.reshape(B, T, GLA_HEADS, GLA_DK) * (GLA_DK ** -0.5)
    k = k.reshape(B, T, GLA_HEADS, GLA_DK)
    v = v.reshape(B, T, GLA_HEADS, GLA_DV)
    log_a = jax.nn.log_sigmoid((a_lr @ w_a2 + b_a).astype(jnp.float32)) / GLA_TAU
    log_a = log_a.reshape(B, T, GLA_HEADS, GLA_DK)
    o, s_new = _gla_recurrence(q, k, v, log_a, s0)
    o = o * lax.rsqrt(jnp.mean(jnp.square(o), axis=-1, keepdims=True) + LN_EPS) * norm_g.astype(jnp.float32)
    o = o.reshape(B, T, GLA_DV_TOT).astype(x.dtype) * jax.nn.silu(r)
    return o @ w_out, s_new.astype(s0.dtype)


def _t5_buckets(dist):
    max_exact = N_BUCKETS // 2
    d = np.asarray(dist, dtype=np.int64)
    log_ratio = np.log(np.maximum(d, max_exact) / max_exact) / math.log(BUCKET_MAX_DIST / max_exact)
    large = np.minimum(max_exact + (log_ratio * (N_BUCKETS - max_exact)).astype(np.int64), N_BUCKETS - 1)
    return np.where(d < max_exact, d, large).astype(np.int32)


def _group_bias(rel_bias, g):
    window, dil = DSW_GROUPS[g]
    M = window // dil
    buckets = _t5_buckets(dil * np.arange(M + 1))
    return rel_bias[buckets][:, g * DSW_HEADS:(g + 1) * DSW_HEADS].astype(jnp.float32)


def _dilated_banded(q, k, v, bias_g, window, dil):
    B, T, H, hd = q.shape
    M = window // dil
    L = T // dil
    bq = math.gcd(L, 128)
    nb = L // bq

    def split(x):
        return x.reshape(B, L, dil, H, hd).transpose(0, 2, 1, 3, 4)

    qb = split(q).reshape(B, dil, nb, bq, H, hd)
    pad = ((0, 0), (0, 0), (M, 0), (0, 0), (0, 0))
    kp = jnp.pad(split(k), pad)
    vp = jnp.pad(split(v), pad)
    idx = np.arange(nb)[:, None] * bq + np.arange(bq + M)[None, :]
    kb = kp[:, :, idx]
    vb = vp[:, :, idx]
    m = np.arange(bq)[:, None] + M - np.arange(bq + M)[None, :]
    mask = ((m >= 0) & (m <= M))[None, :, :] & (idx >= M)[:, None, :]
    bias = jnp.transpose(bias_g[np.clip(m, 0, M)], (2, 0, 1))
    s = jnp.einsum('bgnqhd,bgnkhd->bgnhqk', qb, kb, preferred_element_type=jnp.float32) + bias
    s = jnp.where(mask[None, None, :, None], s, NEG_INF)
    mx = jnp.max(s, axis=-1, keepdims=True)
    p = jnp.exp(s - mx)
    den = jnp.sum(p, axis=-1, keepdims=True)
    o = jnp.einsum('bgnhqk,bgnkhd->bgnqhd', p, vb, preferred_element_type=jnp.float32) / jnp.transpose(den, (0, 1, 2, 4, 3, 5))
    lse = (mx + jnp.log(den))[..., 0]
    o = o.reshape(B, dil, L, H, hd).transpose(0, 2, 1, 3, 4).reshape(B, T, H, hd)
    lse = jnp.transpose(lse, (0, 1, 2, 4, 3)).reshape(B, dil, L, H).transpose(0, 2, 1, 3).reshape(B, T, H)
    return o, lse


def _dilated_gather(q, k_all, v_all, q_off, bias_g, window, dil):
    Tq = q.shape[1]
    M = window // dil
    idx = q_off + np.arange(Tq)[:, None] - dil * np.arange(M + 1)[None, :]
    valid = idx >= 0
    safe = np.clip(idx, 0, None)
    kg = k_all[:, safe]
    vg = v_all[:, safe]
    s = jnp.einsum('bqhd,bqmhd->bhqm', q, kg, preferred_element_type=jnp.float32) + jnp.transpose(bias_g)[None, :, None, :]
    s = jnp.where(valid[None, None], s, NEG_INF)
    mx = jnp.max(s, axis=-1, keepdims=True)
    p = jnp.exp(s - mx)
    den = jnp.sum(p, axis=-1, keepdims=True)
    o = jnp.einsum('bhqm,bqmhd->bqhd', p, vg, preferred_element_type=jnp.float32) / jnp.transpose(den, (0, 2, 1, 3))
    lse = jnp.transpose((mx + jnp.log(den))[..., 0], (0, 2, 1))
    return o, lse


def _dsw_mixer(h, k_all, v_all, q_off, banded, w_q, w_out, rel_bias):
    B, T, _ = h.shape
    q = (h @ w_q).reshape(B, T, N_GROUPS, DSW_HEADS, DSW_HD) * (DSW_HD ** -0.5)
    outs, lses = [], []
    for g, (window, dil) in enumerate(DSW_GROUPS):
        bias_g = _group_bias(rel_bias, g)
        if banded:
            o, l = _dilated_banded(q[:, :, g], k_all, v_all, bias_g, window, dil)
        else:
            o, l = _dilated_gather(q[:, :, g], k_all, v_all, q_off, bias_g, window, dil)
        outs.append(o)
        lses.append(l)
    w = jax.nn.softmax(jnp.stack(lses), axis=0)
    o = jnp.sum(jnp.stack(outs) * w[..., None], axis=0)
    return o.reshape(B, T, DSW_HEADS * DSW_HD).astype(h.dtype) @ w_out


def _trunk(x, gla_s0, k_past, v_past, ln_g, ln_b, gla_w_in, gla_w_a2, gla_b_a, gla_norm_g, gla_w_out,
           kv_w, dsw_w_q, dsw_w_out, rel_bias, ffn_w_gu, ffn_w_down, moe_w_router, moe_w_gu, moe_w_down):
    B, T, _ = x.shape
    n_a = DEPTH // 2
    h = x
    new_s = []
    k_sh = v_sh = None
    k_all = v_all = None
    q_off, banded = 0, True
    for layer in range(DEPTH):
        if layer < n_a:
            s0 = jnp.zeros((B, GLA_HEADS, GLA_DK, GLA_DV), x.dtype) if gla_s0 is None else gla_s0[layer]
            mix, s_new = _gla_mixer(h, s0, gla_w_in[layer], gla_w_a2[layer], gla_b_a[layer],
                                    gla_norm_g[layer], gla_w_out[layer])
            new_s.append(s_new)
        else:
            if k_sh is None:
                kv = (h @ kv_w).reshape(B, T, 2, DSW_HEADS, DSW_HD)
                k_sh, v_sh = kv[:, :, 0], kv[:, :, 1]
                if k_past is None:
                    k_all, v_all, q_off, banded = k_sh, v_sh, 0, True
                else:
                    k_all = jnp.concatenate([k_past.astype(k_sh.dtype), k_sh], axis=1)
                    v_all = jnp.concatenate([v_past.astype(v_sh.dtype), v_sh], axis=1)
                    q_off, banded = k_past.shape[1], False
            j = layer - n_a
            mix = _dsw_mixer(h, k_all, v_all, q_off, banded, dsw_w_q[j], dsw_w_out[j], rel_bias)
        h = _layer_norm(ALPHA * h + mix, ln_g[layer, 0], ln_b[layer, 0])
        if layer % 2 == 0:
            f = _swiglu(h, ffn_w_gu[layer // 2], ffn_w_down[layer // 2])
        else:
            f = _moe(h, moe_w_router[layer // 2], moe_w_gu[layer // 2], moe_w_down[layer // 2])
        h = _layer_norm(ALPHA * h + f, ln_g[layer, 1], ln_b[layer, 1])
    return h, jnp.stack(new_s), k_sh, v_sh


def setup_inputs(seed: int = 0) -> dict:
    key = jax.random.key(seed)
    ks = jax.random.split(key, 24)
    f32 = jnp.float32

    def nrm(k, shape, scale):
        return jax.random.normal(k, shape, f32) * scale

    n_a = DEPTH // 2
    n_b = DEPTH - n_a
    n_dense = (DEPTH + 1) // 2
    n_moe = DEPTH // 2
    l_buf = min(DSW_WINDOW_MAX, PAST_LEN)
    return {
        'x_prompt': nrm(ks[0], (BATCH, SEQ, D_MODEL), 1.0),
        'x_sample': nrm(ks[1], (DEC_BATCH, DEC_SEQ, D_MODEL), 1.0),
        'state_gla': nrm(ks[2], (n_a, DEC_BATCH, GLA_HEADS, GLA_DK, GLA_DV), 1.0),
        'cache_k': nrm(ks[3], (DEC_BATCH, l_buf, DSW_HEADS, DSW_HD), 1.0),
        'cache_v': nrm(ks[4], (DEC_BATCH, l_buf, DSW_HEADS, DSW_HD), 1.0),
        'ln_g': 1.0 + nrm(ks[5], (DEPTH, 2, D_MODEL), 0.02),
        'ln_b': nrm(ks[6], (DEPTH, 2, D_MODEL), 0.02),
        'gla_w_in': nrm(ks[7], (n_a, D_MODEL, GLA_IN), D_MODEL ** -0.5),
        'gla_w_a2': nrm(ks[8], (n_a, GLA_RANK, GLA_DK_TOT), GLA_RANK ** -0.5),
        'gla_b_a': nrm(ks[9], (n_a, GLA_DK_TOT), 0.02),
        'gla_norm_g': 1.0 + nrm(ks[10], (n_a, GLA_DV), 0.02),
        'gla_w_out': nrm(ks[11], (n_a, GLA_DV_TOT, D_MODEL), BETA * GLA_DV_TOT ** -0.5),
        'kv_w': nrm(ks[12], (D_MODEL, 2 * DSW_HEADS * DSW_HD), D_MODEL ** -0.5),
        'dsw_w_q': nrm(ks[13], (n_b, D_MODEL, N_GROUPS * DSW_HEADS * DSW_HD), D_MODEL ** -0.5),
        'dsw_w_out': nrm(ks[14], (n_b, DSW_HEADS * DSW_HD, D_MODEL), BETA * (DSW_HEADS * DSW_HD) ** -0.5),
        'rel_bias': nrm(ks[15], (N_BUCKETS, N_GROUPS * DSW_HEADS), 0.2),
        'ffn_w_gu': nrm(ks[16], (n_dense, D_MODEL, 2 * D_FF_DENSE), D_MODEL ** -0.5),
        'ffn_w_down': nrm(ks[17], (n_dense, D_FF_DENSE, D_MODEL), BETA * D_FF_DENSE ** -0.5),
        'moe_w_router': nrm(ks[18], (n_moe, D_MODEL, N_EXPERTS), D_MODEL ** -0.5),
        'moe_w_gu': nrm(ks[19], (n_moe, N_EXPERTS, D_MODEL, 2 * D_FF_EXPERT), D_MODEL ** -0.5),
        'moe_w_down': nrm(ks[20], (n_moe, N_EXPERTS, D_FF_EXPERT, D_MODEL), BETA * D_FF_EXPERT ** -0.5),
    }


def reference(x_prompt, x_sample, state_gla, cache_k, cache_v, ln_g, ln_b, gla_w_in, gla_w_a2, gla_b_a,
              gla_norm_g, gla_w_out, kv_w, dsw_w_q, dsw_w_out, rel_bias, ffn_w_gu, ffn_w_down,
              moe_w_router, moe_w_gu, moe_w_down):
    weights = (ln_g, ln_b, gla_w_in, gla_w_a2, gla_b_a, gla_norm_g, gla_w_out, kv_w, dsw_w_q, dsw_w_out,
               rel_bias, ffn_w_gu, ffn_w_down, moe_w_router, moe_w_gu, moe_w_down)
    y_prompt, state_gla_prompt, k_p, v_p = _trunk(x_prompt, None, None, None, *weights)
    keep = min(DSW_WINDOW_MAX, x_prompt.shape[1])
    cache_k_prompt = k_p[:, x_prompt.shape[1] - keep:]
    cache_v_prompt = v_p[:, x_prompt.shape[1] - keep:]
    y_sample, state_gla_sample, cache_k_sample, cache_v_sample = _trunk(x_sample, state_gla, cache_k, cache_v, *weights)
    return (y_prompt, y_sample, state_gla_prompt, state_gla_sample,
            cache_k_prompt, cache_v_prompt, cache_k_sample, cache_v_sample)
```

```python
import functools
import math

import jax
import jax.numpy as jnp
import numpy as np
from jax import lax
from jax.experimental import pallas as pl
from jax.experimental.pallas import tpu as pltpu

F32 = jnp.float32
BF16 = jnp.bfloat16

GLA_TAU = 16.0
GLA_CHUNK = 64
LN_EPS = 1e-5
NEG_INF = -1e30
DSW_GROUPS = ((128, 1), (512, 4), (2048, 16))
N_BUCKETS = 32
TOP_K = 2

LANES = 128
VMEM_LIMIT_BYTES = 56 * 1024 * 1024
ATTN_BLOCK = 128


def _params(*sem):
    return pltpu.CompilerParams(dimension_semantics=sem, vmem_limit_bytes=VMEM_LIMIT_BYTES)


def _layer_norm(x, g, b):
    mu = jnp.mean(x, axis=-1, keepdims=True)
    xc = x - mu
    var = jnp.mean(xc * xc, axis=-1, keepdims=True)
    return xc * lax.rsqrt(var + LN_EPS) * g + b


def _dot(a, b):
    return jnp.dot(a, b, preferred_element_type=F32)


def _dot_nt(a, b):
    return lax.dot_general(a, b, (((1,), (1,)), ((), ())), preferred_element_type=F32)


def _dot_tn(a, b):
    return lax.dot_general(a, b, (((0,), (0,)), ((), ())), preferred_element_type=F32)


def _dot_exact(a, b):
    return jnp.dot(a, b, preferred_element_type=F32, precision=lax.Precision.HIGHEST)


def _split3(x):
    hi = x.astype(BF16)
    r1 = x - hi.astype(F32)
    mid = r1.astype(BF16)
    lo = (r1 - mid.astype(F32)).astype(BF16)
    return hi, mid, lo


def _mm_kernel(a_ref, w_ref, o_ref, *, scale):
    acc = _dot(a_ref[...], w_ref[...])
    if scale != 1.0:
        acc = acc * scale
    o_ref[...] = acc.astype(o_ref.dtype)


def _mm(a, w, out_dtype=F32, scale=1.0, tm=1024, tn=1024, name="mm"):
    M, K = a.shape
    N = w.shape[1]
    tm = min(tm, M)
    tn = math.gcd(tn, N)
    assert M % tm == 0 and N % tn == 0, (M, N, tm, tn)
    return pl.pallas_call(
        functools.partial(_mm_kernel, scale=scale),
        grid=(M // tm, N // tn),
        in_specs=[pl.BlockSpec((tm, K), lambda i, j: (i, 0)),
                  pl.BlockSpec((K, tn), lambda i, j: (0, j))],
        out_specs=pl.BlockSpec((tm, tn), lambda i, j: (i, j)),
        out_shape=jax.ShapeDtypeStruct((M, N), out_dtype),
        compiler_params=_params("parallel", "arbitrary"),
        name=name,
    )(a, w)


def _mm_res_ln_kernel(a_ref, w_ref, res_ref, g_ref, b_ref, o_ref, obf_ref, *, alpha):
    mix = _dot(a_ref[...], w_ref[...])
    y = _layer_norm(alpha * res_ref[...] + mix, g_ref[...], b_ref[...])
    o_ref[...] = y
    obf_ref[...] = y.astype(BF16)


def _mm_res_ln(a, w, res, g, b, alpha, tm=512, name="mm_res_ln"):
    M, K = a.shape
    N = w.shape[1]
    tm = min(tm, M)
    assert M % tm == 0
    row = lambda i: (i, 0)
    fixed = lambda i: (0, 0)
    return pl.pallas_call(
        functools.partial(_mm_res_ln_kernel, alpha=alpha),
        grid=(M // tm,),
        in_specs=[pl.BlockSpec((tm, K), row), pl.BlockSpec((K, N), fixed),
                  pl.BlockSpec((tm, N), row), pl.BlockSpec((1, N), fixed), pl.BlockSpec((1, N), fixed)],
        out_specs=[pl.BlockSpec((tm, N), row), pl.BlockSpec((tm, N), row)],
        out_shape=[jax.ShapeDtypeStruct((M, N), F32), jax.ShapeDtypeStruct((M, N), BF16)],
        compiler_params=_params("parallel"),
        name=name,
    )(a, w, res, g.reshape(1, N), b.reshape(1, N))


def _ffn_kernel(te_ref, used_ref, x_ref, wg_ref, wu_ref, wd_ref, *rest, mode, alpha, nf):
    if mode == "ln":
        res_ref, g_ref, b_ref, o_ref, obf_ref = rest
    else:
        gate_ref, o_ref = rest
    i = pl.program_id(0)
    f = pl.program_id(1)

    @pl.when(f == 0)
    def _():
        o_ref[...] = jnp.zeros_like(o_ref)

    @pl.when(used_ref[i] != 0)
    def _():
        x = x_ref[...].astype(BF16)
        g = _dot(x, wg_ref[0])
        u = _dot(x, wu_ref[0])
        act = (g * jax.nn.sigmoid(g) * u).astype(BF16)
        o_ref[...] += _dot(act, wd_ref[0])

    @pl.when(f == nf - 1)
    def _():
        if mode == "ln":
            y = _layer_norm(alpha * res_ref[...] + o_ref[...], g_ref[...], b_ref[...])
            o_ref[...] = y
            obf_ref[...] = y.astype(BF16)
        else:
            o_ref[...] = o_ref[...] * gate_ref[...]


def _ffn(x, w_gu, w_down, tile_expert, tile_used, *, tm, tf, mode, alpha=1.0,
         res=None, g=None, b=None, gate=None, name="ffn"):
    M, D = x.shape
    E, F, _ = w_down.shape
    assert M % tm == 0 and F % tf == 0
    nf = F // tf
    row = lambda i, f, te, us: (i, 0)
    fixed = lambda i, f, te, us: (0, 0)
    in_specs = [
        pl.BlockSpec((tm, D), row),
        pl.BlockSpec((1, D, tf), lambda i, f, te, us: (te[i], 0, f * us[i])),
        pl.BlockSpec((1, D, tf), lambda i, f, te, us: (te[i], 0, nf + f * us[i])),
        pl.BlockSpec((1, tf, D), lambda i, f, te, us: (te[i], f * us[i], 0)),
    ]
    if mode == "ln":
        in_specs += [pl.BlockSpec((tm, D), row), pl.BlockSpec((1, D), fixed), pl.BlockSpec((1, D), fixed)]
        extra = (res, g.reshape(1, D), b.reshape(1, D))
        out_specs = [pl.BlockSpec((tm, D), row), pl.BlockSpec((tm, D), row)]
        out_shape = [jax.ShapeDtypeStruct((M, D), F32), jax.ShapeDtypeStruct((M, D), BF16)]
    else:
        in_specs += [pl.BlockSpec((tm, 1), row)]
        extra = (gate.reshape(M, 1),)
        out_specs = pl.BlockSpec((tm, D), row)
        out_shape = jax.ShapeDtypeStruct((M, D), F32)
    return pl.pallas_call(
        functools.partial(_ffn_kernel, mode=mode, alpha=alpha, nf=nf),
        grid_spec=pltpu.PrefetchScalarGridSpec(
            num_scalar_prefetch=2, grid=(M // tm, nf), in_specs=in_specs, out_specs=out_specs),
        out_shape=out_shape,
        compiler_params=_params("parallel", "arbitrary"),
        name=name,
    )(tile_expert, tile_used, x, w_gu, w_gu, w_down, *extra)


def _log_sigmoid(z):
    return jnp.minimum(z, 0.0) - jnp.log(1.0 + jnp.exp(-jnp.abs(z)))


def _gla_prompt_kernel(q_ref, k_ref, v_ref, r_ref, alr_ref, wa_ref, ba_ref, ng_ref, o_ref, s_ref,
                       *, nchunk, scale):
    C = GLA_CHUNK
    dk = q_ref.shape[1]

    @pl.when(pl.program_id(2) == 0)
    def _():
        s_ref[...] = jnp.zeros_like(s_ref)

    rows = lax.broadcasted_iota(jnp.int32, (C, dk), 0)
    causal = lax.broadcasted_iota(jnp.int32, (C, C), 0) >= lax.broadcasted_iota(jnp.int32, (C, C), 1)
    for j in range(nchunk):
        sl = slice(j * C, (j + 1) * C)
        z = _dot_exact(alr_ref[sl, :], wa_ref[...]) + ba_ref[...]
        bcum = _log_sigmoid(z) / GLA_TAU
        shift = 1
        while shift < C:
            bcum = bcum + jnp.where(rows >= shift, pltpu.roll(bcum, shift, 0), 0.0)
            shift *= 2
        b_last = bcum[C - 1:C, :]
        kc = k_ref[sl, :]
        vc = v_ref[sl, :].astype(BF16)
        q_dec = (q_ref[sl, :] * scale * jnp.exp(bcum)).astype(BF16)
        k_dec = (kc * jnp.exp(-bcum)).astype(BF16)
        k_end = (kc * jnp.exp(b_last - bcum)).astype(BF16)
        a = jnp.where(causal, _dot_nt(q_dec, k_dec), 0.0)
        s_old = s_ref[0, 0]
        o = _dot(q_dec, s_old.astype(BF16)) + _dot(a.astype(BF16), vc)
        decay = jnp.transpose(jnp.broadcast_to(jnp.exp(b_last), (LANES, dk)))[:, 0:1]
        s_ref[0, 0] = s_old * decay + _dot_tn(k_end, vc)
        on = o * lax.rsqrt(jnp.mean(o * o, axis=-1, keepdims=True) + LN_EPS) * ng_ref[...]
        rr = r_ref[sl, :]
        o_ref[sl, :] = (on * (rr * jax.nn.sigmoid(rr))).astype(o_ref.dtype)


def _gla_prompt(qkvr, alr, w_a2p, b_a, norm_g, *, B, T, H, dk, dv, cb=256):
    N = B * T
    cb = min(cb, T)
    assert T % cb == 0 and cb % GLA_CHUNK == 0
    nc = T // cb
    dkt = H * dk
    dvt = H * dv
    assert (2 * dkt) % dv == 0
    v0 = (2 * dkt) // dv
    r0 = (2 * dkt + dvt) // dv
    rowblk = lambda b, h, c: b * nc + c
    return pl.pallas_call(
        functools.partial(_gla_prompt_kernel, nchunk=cb // GLA_CHUNK, scale=dk ** -0.5),
        grid=(B, H, nc),
        in_specs=[
            pl.BlockSpec((cb, dk), lambda b, h, c: (rowblk(b, h, c), h)),
            pl.BlockSpec((cb, dk), lambda b, h, c: (rowblk(b, h, c), H + h)),
            pl.BlockSpec((cb, dv), lambda b, h, c: (rowblk(b, h, c), v0 + h)),
            pl.BlockSpec((cb, dv), lambda b, h, c: (rowblk(b, h, c), r0 + h)),
            pl.BlockSpec((cb, LANES), lambda b, h, c: (rowblk(b, h, c), 0)),
            pl.BlockSpec((LANES, dk), lambda b, h, c: (0, h)),
            pl.BlockSpec((1, dk), lambda b, h, c: (0, h)),
            pl.BlockSpec((1, dv), lambda b, h, c: (0, 0)),
        ],
        out_specs=[
            pl.BlockSpec((cb, dv), lambda b, h, c: (rowblk(b, h, c), h)),
            pl.BlockSpec((1, 1, dk, dv), lambda b, h, c: (b, h, 0, 0)),
        ],
        out_shape=[jax.ShapeDtypeStruct((N, dvt), BF16), jax.ShapeDtypeStruct((B, H, dk, dv), F32)],
        compiler_params=_params("parallel", "parallel", "arbitrary"),
        name="gla_prompt",
    )(qkvr, qkvr, qkvr, qkvr, alr, w_a2p, b_a.reshape(1, dkt), norm_g.reshape(1, dv))


def _gla_sample_kernel(q_ref, k_ref, v_ref, r_ref, alr_ref, wat_ref, ba_ref, ng_ref, s0_ref, o_ref, s_ref,
                       *, scale):
    z = jnp.sum(wat_ref[...] * alr_ref[0], axis=-1, keepdims=True) + ba_ref[...]
    decay = jnp.exp(_log_sigmoid(z) / GLA_TAU)
    s_new = s0_ref[0, 0] * decay + k_ref[0, 0] * v_ref[0]
    s_ref[0, 0] = s_new
    o = jnp.sum((q_ref[0, 0] * scale) * s_new, axis=0, keepdims=True)
    on = o * lax.rsqrt(jnp.mean(o * o, axis=-1, keepdims=True) + LN_EPS) * ng_ref[...]
    rr = r_ref[0]
    o_ref[0] = (on * (rr * jax.nn.sigmoid(rr))).astype(o_ref.dtype)


def _gla_sample(qkvr, alr, w_a2tp, b_a, norm_g, s0, *, H, dk, dv):
    Bd = qkvr.shape[0]
    dkt = H * dk
    dvt = H * dv
    q = qkvr[:, :dkt].reshape(Bd, H, dk, 1)
    k = qkvr[:, dkt:2 * dkt].reshape(Bd, H, dk, 1)
    v = qkvr[:, 2 * dkt:2 * dkt + dvt].reshape(Bd, 1, dvt)
    r = qkvr[:, 2 * dkt + dvt:].reshape(Bd, 1, dvt)
    col = lambda b, h: (b, h, 0, 0)
    return pl.pallas_call(
        functools.partial(_gla_sample_kernel, scale=dk ** -0.5),
        grid=(Bd, H),
        in_specs=[
            pl.BlockSpec((1, 1, dk, 1), col),
            pl.BlockSpec((1, 1, dk, 1), col),
            pl.BlockSpec((1, 1, dv), lambda b, h: (b, 0, h)),
            pl.BlockSpec((1, 1, dv), lambda b, h: (b, 0, h)),
            pl.BlockSpec((1, 1, LANES), lambda b, h: (b, 0, 0)),
            pl.BlockSpec((dk, LANES), lambda b, h: (h, 0)),
            pl.BlockSpec((dk, 1), lambda b, h: (h, 0)),
            pl.BlockSpec((1, dv), lambda b, h: (0, 0)),
            pl.BlockSpec((1, 1, dk, dv), col),
        ],
        out_specs=[
            pl.BlockSpec((1, 1, dv), lambda b, h: (b, 0, h)),
            pl.BlockSpec((1, 1, dk, dv), col),
        ],
        out_shape=[jax.ShapeDtypeStruct((Bd, 1, dvt), BF16), jax.ShapeDtypeStruct((Bd, H, dk, dv), F32)],
        compiler_params=_params("parallel", "parallel"),
        name="gla_sample",
    )(q, k, v, r, alr.reshape(Bd, 1, LANES), w_a2tp, b_a.reshape(dkt, 1), norm_g.reshape(1, dv), s0)


def _attn_prompt_kernel(q0_ref, q1_ref, q2_ref, k_ref, v_ref, bias_ref, o_ref, acc_ref, m_ref, l_ref,
                        *, T, dils):
    BQ = ATTN_BLOCK
    q_refs = (q0_ref, q1_ref, q2_ref)
    hd = k_ref.shape[2]

    def rows_of(start, d):
        return pl.ds(start, BQ) if d == 1 else pl.ds(start, BQ, stride=d)

    for g, d in enumerate(dils):
        nb = T // (d * BQ)
        for r in range(d):
            for i in range(nb):
                rows = rows_of(r + i * BQ * d, d)
                qb = q_refs[g][0, rows, :].astype(BF16)
                kc = k_ref[0, rows, :].astype(BF16)
                vc = v_ref[0, rows, :].astype(BF16)
                s_cur = _dot_nt(qb, kc) + bias_ref[g, 0, :, BQ:2 * BQ]
                m_new = jnp.max(s_cur, axis=-1, keepdims=True)
                if i > 0:
                    prow = rows_of(r + (i - 1) * BQ * d, d)
                    kp = k_ref[0, prow, :].astype(BF16)
                    vp = v_ref[0, prow, :].astype(BF16)
                    s_prev = _dot_nt(qb, kp) + bias_ref[g, 0, :, 0:BQ]
                    m_new = jnp.maximum(m_new, jnp.max(s_prev, axis=-1, keepdims=True))
                if g > 0:
                    m_old = m_ref[rows, :][:, 0:1]
                    m_new = jnp.maximum(m_new, m_old)
                p = jnp.exp(s_cur - m_new)
                l_new = jnp.sum(p, axis=-1, keepdims=True)
                pv = _dot(p.astype(BF16), vc)
                if i > 0:
                    pp = jnp.exp(s_prev - m_new)
                    l_new = l_new + jnp.sum(pp, axis=-1, keepdims=True)
                    pv = pv + _dot(pp.astype(BF16), vp)
                if g > 0:
                    alpha = jnp.exp(m_old - m_new)
                    l_new = l_new + alpha * l_ref[rows, :][:, 0:1]
                    pv = pv + alpha * acc_ref[rows, :]
                m_ref[rows, :] = jnp.broadcast_to(m_new, (BQ, hd))
                l_ref[rows, :] = jnp.broadcast_to(l_new, (BQ, hd))
                acc_ref[rows, :] = pv
    o_ref[0] = (acc_ref[...] / l_ref[...]).astype(o_ref.dtype)


def _attn_prompt(q, k, v, bias_tiles, *, B, T, H, hd):
    dils = tuple(d for _, d in DSW_GROUPS)
    G = len(dils)
    for w, d in DSW_GROUPS:
        assert w // d == ATTN_BLOCK and T % (d * ATTN_BLOCK) == 0
    assert hd == LANES
    qspec = lambda g: pl.BlockSpec((1, T, hd), lambda b, h: (b, 0, g * H + h))
    kvspec = pl.BlockSpec((1, T, hd), lambda b, h: (b, 0, h))
    return pl.pallas_call(
        functools.partial(_attn_prompt_kernel, T=T, dils=dils),
        grid=(B, H),
        in_specs=[qspec(0), qspec(1), qspec(2), kvspec, kvspec,
                  pl.BlockSpec((G, 1, ATTN_BLOCK, 2 * ATTN_BLOCK), lambda b, h: (0, h, 0, 0))],
        out_specs=pl.BlockSpec((1, T, hd), lambda b, h: (b, 0, h)),
        out_shape=jax.ShapeDtypeStruct((B, T, H * hd), BF16),
        scratch_shapes=[pltpu.VMEM((T, hd), F32), pltpu.VMEM((T, hd), F32), pltpu.VMEM((T, hd), F32)],
        compiler_params=_params("parallel", "parallel"),
        name="attn_prompt",
    )(q, q, q, k, v, bias_tiles)


def _attn_sample_kernel(q_ref, kn_ref, vn_ref, ck0, ck1, ck2, cv0, cv1, cv2, bias_ref, e_ref, et_ref, o_ref,
                        *, G, D):
    cks = (ck0, ck1, ck2)
    cvs = (cv0, cv1, cv2)
    BQ = ATTN_BLOCK
    e = e_ref[...]
    et = et_ref[...]

    def head_sums(x):
        hi, mid, _ = _split3(x)
        return _dot(hi, e) + _dot(mid, e)

    def expand(x):
        return _dot(x, et)

    s_cache, s_new = [], []
    for g in range(G):
        qg = q_ref[0, :, g * D:(g + 1) * D]
        s_cache.append(head_sums(cks[g][0] * qg) + bias_ref[g, 0:BQ, :])
        s_new.append(head_sums(kn_ref[0] * qg) + bias_ref[g, BQ:BQ + 1, :])
    mx = s_new[0]
    for g in range(G):
        mx = jnp.maximum(mx, jnp.maximum(s_new[g], jnp.max(s_cache[g], axis=0, keepdims=True)))
    den = jnp.zeros_like(mx)
    o = jnp.zeros((1, D), F32)
    for g in range(G):
        pc = jnp.exp(s_cache[g] - mx)
        pn = jnp.exp(s_new[g] - mx)
        den = den + jnp.sum(pc, axis=0, keepdims=True) + pn
        o = o + jnp.sum(expand(pc.astype(BF16)) * cvs[g][0], axis=0, keepdims=True)
        o = o + expand(pn.astype(BF16)) * vn_ref[0]
    dh, dm, dl = _split3(den)
    den_full = expand(dh) + expand(dm) + expand(dl)
    o_ref[0] = (o / den_full).astype(o_ref.dtype)


def _attn_sample(q, k_new, v_new, cache_k, cache_v, bias_rows, *, H, hd):
    Bd, Lc = cache_k.shape[:2]
    D = H * hd
    G = len(DSW_GROUPS)
    BQ = ATTN_BLOCK
    views_k, views_v, specs = [], [], []
    for w, d in DSW_GROUPS:
        assert w // d == BQ and Lc % (d * BQ) == 0
        views_k.append(cache_k.reshape(Bd, Lc // d, d * D))
        views_v.append(cache_v.reshape(Bd, Lc // d, d * D))
        last = Lc // d // BQ - 1
        specs.append(pl.BlockSpec((1, BQ, D), functools.partial(lambda b, last: (b, last, 0), last=last)))
    head_of_lane = np.arange(D) // hd
    e = jnp.asarray((head_of_lane[:, None] == np.arange(LANES)[None, :]), dtype=BF16)
    et = jnp.asarray((np.arange(LANES)[:, None] == head_of_lane[None, :]), dtype=BF16)
    one = lambda b: (b, 0, 0)
    nb_rows = bias_rows.shape[1]
    return pl.pallas_call(
        functools.partial(_attn_sample_kernel, G=G, D=D),
        grid=(Bd,),
        in_specs=[pl.BlockSpec((1, 1, G * D), one), pl.BlockSpec((1, 1, D), one), pl.BlockSpec((1, 1, D), one),
                  *specs, *specs,
                  pl.BlockSpec((G, nb_rows, LANES), lambda b: (0, 0, 0)),
                  pl.BlockSpec((D, LANES), lambda b: (0, 0)),
                  pl.BlockSpec((LANES, D), lambda b: (0, 0))],
        out_specs=pl.BlockSpec((1, 1, D), one),
        out_shape=jax.ShapeDtypeStruct((Bd, 1, D), BF16),
        compiler_params=_params("parallel"),
        name="attn_sample",
    )(q.reshape(Bd, 1, G * D), k_new.reshape(Bd, 1, D), v_new.reshape(Bd, 1, D),
      *views_k, *views_v, bias_rows, e, et)


def _router_kernel(h_ref, w_ref, o_ref, *, n_experts):
    logits = _dot_exact(h_ref[...], w_ref[...])
    lane = lax.broadcasted_iota(jnp.int32, logits.shape, 1)
    lanef = lane.astype(F32)
    logits = jnp.where(lane < n_experts, logits, -jnp.inf)
    v1 = jnp.max(logits, axis=-1, keepdims=True)
    i1 = jnp.min(jnp.where(logits == v1, lanef, float(LANES)), axis=-1, keepdims=True)
    rest = jnp.where(lanef == i1, -jnp.inf, logits)
    v2 = jnp.max(rest, axis=-1, keepdims=True)
    i2 = jnp.min(jnp.where(rest == v2, lanef, float(LANES)), axis=-1, keepdims=True)
    e2 = jnp.exp(v2 - v1)
    w1 = 1.0 / (1.0 + e2)
    w2 = e2 / (1.0 + e2)
    o_ref[...] = jnp.where(lane == 0, i1, jnp.where(lane == 1, i2, jnp.where(lane == 2, w1, w2)))


def _router(h, w_router_p, n_experts, tm=512):
    M, D = h.shape
    tm = min(tm, M)
    assert M % tm == 0
    out = pl.pallas_call(
        functools.partial(_router_kernel, n_experts=n_experts),
        grid=(M // tm,),
        in_specs=[pl.BlockSpec((tm, D), lambda i: (i, 0)), pl.BlockSpec((D, LANES), lambda i: (0, 0))],
        out_specs=pl.BlockSpec((tm, LANES), lambda i: (i, 0)),
        out_shape=jax.ShapeDtypeStruct((M, LANES), F32),
        compiler_params=_params("parallel"),
        name="router",
    )(h, w_router_p)
    return out[:, 0:2].astype(jnp.int32), out[:, 2:4]


def _row_copy(src_ref, dst_ref, sem, src_row, dst_row):
    return pltpu.make_async_copy(src_ref.at[pl.ds(src_row, 1), :], dst_ref.at[pl.ds(dst_row, 1), :], sem)


def _gather_kernel(idx_ref, src_ref, o_ref, sem, *, R):
    base = pl.program_id(0) * R

    def start(r, c):
        _row_copy(src_ref, o_ref, sem, idx_ref[base + r], r).start()
        return c

    lax.fori_loop(0, R, start, 0)

    def wait(r, c):
        _row_copy(src_ref, o_ref, sem, 0, r).wait()
        return c

    lax.fori_loop(0, R, wait, 0)


def _gather_rows(src, idx, R=256):
    M = idx.shape[0]
    D = src.shape[1]
    R = min(R, M)
    assert M % R == 0
    return pl.pallas_call(
        functools.partial(_gather_kernel, R=R),
        grid_spec=pltpu.PrefetchScalarGridSpec(
            num_scalar_prefetch=1, grid=(M // R,),
            in_specs=[pl.BlockSpec(memory_space=pl.ANY)],
            out_specs=pl.BlockSpec((R, D), lambda i, idx: (i, 0)),
            scratch_shapes=[pltpu.SemaphoreType.DMA(())]),
        out_shape=jax.ShapeDtypeStruct((M, D), src.dtype),
        compiler_params=_params("arbitrary"),
        name="moe_dispatch",
    )(idx, src)


def _combine_ln_kernel(p0_ref, p1_ref, rows_ref, h_ref, g_ref, b_ref, o_ref, buf0, buf1, sem, *, R, alpha):
    base = pl.program_id(0) * R

    def start(r, c):
        _row_copy(rows_ref, buf0, sem, p0_ref[base + r], r).start()
        _row_copy(rows_ref, buf1, sem, p1_ref[base + r], r).start()
        return c

    lax.fori_loop(0, R, start, 0)

    def wait(r, c):
        _row_copy(rows_ref, buf0, sem, 0, r).wait()
        _row_copy(rows_ref, buf1, sem, 0, r).wait()
        return c

    lax.fori_loop(0, R, wait, 0)
    o_ref[...] = _layer_norm(alpha * h_ref[...] + buf0[...] + buf1[...], g_ref[...], b_ref[...])


def _combine_ln(rows, p0, p1, h, g, b, alpha, R=256):
    M, D = h.shape
    R = min(R, M)
    assert M % R == 0
    row = lambda i, p0, p1: (i, 0)
    fixed = lambda i, p0, p1: (0, 0)
    return pl.pallas_call(
        functools.partial(_combine_ln_kernel, R=R, alpha=alpha),
        grid_spec=pltpu.PrefetchScalarGridSpec(
            num_scalar_prefetch=2, grid=(M // R,),
            in_specs=[pl.BlockSpec(memory_space=pl.ANY), pl.BlockSpec((R, D), row),
                      pl.BlockSpec((1, D), fixed), pl.BlockSpec((1, D), fixed)],
            out_specs=pl.BlockSpec((R, D), row),
            scratch_shapes=[pltpu.VMEM((R, D), F32), pltpu.VMEM((R, D), F32), pltpu.SemaphoreType.DMA(())]),
        out_shape=jax.ShapeDtypeStruct((M, D), F32),
        compiler_params=_params("arbitrary"),
        name="moe_combine_ln",
    )(p0, p1, rows, h, g.reshape(1, D), b.reshape(1, D))


def _routing_tables(top_i, top_w, n_experts, tm):
    nt = top_i.shape[0]
    na = TOP_K * nt
    e = top_i.reshape(na)
    w = top_w.reshape(na)
    onehot = (e[:, None] == jnp.arange(n_experts, dtype=jnp.int32)[None, :]).astype(jnp.int32)
    csum = jnp.cumsum(onehot, axis=0)
    rank = jnp.take_along_axis(csum, e[:, None], axis=1)[:, 0] - 1
    counts = csum[-1]
    padded = ((counts + tm - 1) // tm) * tm
    pend = jnp.cumsum(padded)
    pstart = pend - padded
    slot = pstart[e] + rank
    ntiles = (na + n_experts * (tm - 1) + tm - 1) // tm
    mpad = ntiles * tm
    src = jnp.zeros((mpad,), jnp.int32).at[slot].set(jnp.arange(na, dtype=jnp.int32) // TOP_K)
    gate = jnp.zeros((mpad,), F32).at[slot].set(w)
    tile_start = jnp.arange(ntiles, dtype=jnp.int32) * tm
    used = (tile_start < pend[-1]).astype(jnp.int32)
    te = jnp.minimum(jnp.searchsorted(pend, tile_start, side="right"), n_experts - 1).astype(jnp.int32)
    last_used = jnp.max(jnp.where(used > 0, te, 0))
    te = jnp.where(used > 0, te, last_used)
    slot2 = slot.reshape(nt, TOP_K)
    return src, gate, te, used, slot2[:, 0], slot2[:, 1]


def _t5_bucket_ids(dist, max_dist):
    max_exact = N_BUCKETS // 2
    d = np.asarray(dist, dtype=np.int64)
    log_ratio = np.log(np.maximum(d, max_exact) / max_exact) / math.log(max_dist / max_exact)
    large = np.minimum(max_exact + (log_ratio * (N_BUCKETS - max_exact)).astype(np.int64), N_BUCKETS - 1)
    return np.where(d < max_exact, d, large).astype(np.int32)


def _group_biases(rel_bias, H):
    max_dist = max(w for w, _ in DSW_GROUPS)
    out = []
    for g, (w, d) in enumerate(DSW_GROUPS):
        buckets = _t5_bucket_ids(d * np.arange(w // d + 1), max_dist)
        out.append(rel_bias[buckets][:, g * H:(g + 1) * H].astype(F32))
    return out


def _prompt_bias_tiles(biases):
    BQ = ATTN_BLOCK
    m = np.arange(BQ)[:, None] + BQ - np.arange(2 * BQ)[None, :]
    valid = (m >= 0) & (m <= BQ)
    tiles = []
    for bg in biases:
        t = jnp.where(valid[:, :, None], bg[np.clip(m, 0, BQ)], NEG_INF)
        tiles.append(jnp.transpose(t, (2, 0, 1)))
    return jnp.stack(tiles)


def _sample_bias_rows(biases, H):
    BQ = ATTN_BLOCK
    rows = []
    for bg in biases:
        r = jnp.concatenate([bg[::-1][:BQ], bg[0:1], jnp.zeros((7, H), F32)], axis=0)
        rows.append(jnp.pad(r, ((0, 0), (0, LANES - H))))
    return jnp.stack(rows)


def kernel(x_prompt, x_sample, state_gla, cache_k, cache_v, ln_g, ln_b, gla_w_in, gla_w_a2, gla_b_a,
           gla_norm_g, gla_w_out, kv_w, dsw_w_q, dsw_w_out, rel_bias, ffn_w_gu, ffn_w_down,
           moe_w_router, moe_w_gu, moe_w_down):
    B, T, D = x_prompt.shape
    Bd = x_sample.shape[0]
    depth = ln_g.shape[0]
    assert depth == 2 and x_sample.shape[1] == 1
    alpha = (2 * depth) ** 0.25
    rank, dkt = gla_w_a2.shape[1:]
    dv = gla_norm_g.shape[1]
    dvt = gla_w_out.shape[1]
    Hg = dvt // dv
    dk = dkt // Hg
    G = len(DSW_GROUPS)
    Ha = rel_bias.shape[1] // G
    hd = D // Ha
    n_experts = moe_w_router.shape[2]
    f_dense = ffn_w_down.shape[1]
    f_exp = moe_w_down.shape[2]
    N = B * T

    w_in = gla_w_in[0]
    c_a = 2 * dkt + dvt
    w_qkvr = jnp.concatenate([w_in[:, :c_a], w_in[:, c_a + rank:]], axis=1).astype(BF16)
    w_alr = jnp.pad(w_in[:, c_a:c_a + rank], ((0, 0), (0, LANES - rank))).astype(BF16)
    w_a2p = jnp.pad(gla_w_a2[0], ((0, LANES - rank), (0, 0)))
    w_a2tp = jnp.pad(gla_w_a2[0].T, ((0, 0), (0, LANES - rank)))
    w_gout = gla_w_out[0].astype(BF16)
    w_k = kv_w[:, :D].astype(BF16)
    w_v = kv_w[:, D:].astype(BF16)
    w_q = dsw_w_q[0].astype(BF16)
    w_aout = dsw_w_out[0].astype(BF16)
    w_ffn_gu = ffn_w_gu.astype(BF16)
    w_ffn_down = ffn_w_down.astype(BF16)
    w_moe_gu = moe_w_gu[0].astype(BF16)
    w_moe_down = moe_w_down[0].astype(BF16)
    w_router_p = jnp.pad(moe_w_router[0], ((0, 0), (0, LANES - n_experts)))
    biases = _group_biases(rel_bias, Ha)
    tf_dense = math.gcd(f_dense, 512)
    tf_exp = math.gcd(f_exp, 512)

    def dense_tables(m, tm):
        n = m // tm
        return jnp.zeros((n,), jnp.int32), jnp.ones((n,), jnp.int32)

    def layer0_in(x2):
        xb = x2.astype(BF16)
        return _mm(xb, w_qkvr, name="gla_in_proj"), _mm(xb, w_alr, name="gla_gate_proj")

    def layer0_out(o_gla, x2):
        m = x2.shape[0]
        h1, h1b = _mm_res_ln(o_gla, w_gout, x2, ln_g[0, 0], ln_b[0, 0], alpha, name="gla_out_ln")
        tm = min(512, m)
        te, us = dense_tables(m, tm)
        h2, h2b = _ffn(h1b, w_ffn_gu, w_ffn_down, te, us, tm=tm, tf=tf_dense, mode="ln", alpha=alpha,
                       res=h1, g=ln_g[0, 1], b=ln_b[0, 1], name="ffn_dense")
        k = _mm(h2b, w_k, name="k_proj")
        v = _mm(h2b, w_v, name="v_proj")
        q = _mm(h2b, w_q, scale=hd ** -0.5, name="q_proj")
        return h2, k, v, q

    xp = x_prompt.reshape(N, D)
    qkvr_p, alr_p = layer0_in(xp)
    o_gla_p, state_p = _gla_prompt(qkvr_p, alr_p, w_a2p, gla_b_a[0], gla_norm_g[0],
                                   B=B, T=T, H=Hg, dk=dk, dv=dv)
    h2_p, k_p, v_p, q_p = layer0_out(o_gla_p, xp)
    o_att_p = _attn_prompt(q_p.reshape(B, T, G * D), k_p.reshape(B, T, D), v_p.reshape(B, T, D),
                           _prompt_bias_tiles(biases), B=B, T=T, H=Ha, hd=hd)
    h3_p, _ = _mm_res_ln(o_att_p.reshape(N, D), w_aout, h2_p, ln_g[1, 0], ln_b[1, 0], alpha, name="attn_out_ln")

    xs = x_sample.reshape(Bd, D)
    qkvr_s, alr_s = layer0_in(xs)
    o_gla_s, state_s = _gla_sample(qkvr_s, alr_s, w_a2tp, gla_b_a[0], gla_norm_g[0], state_gla[0],
                                   H=Hg, dk=dk, dv=dv)
    h2_s, k_s, v_s, q_s = layer0_out(o_gla_s.reshape(Bd, dvt), xs)
    o_att_s = _attn_sample(q_s, k_s, v_s, cache_k, cache_v, _sample_bias_rows(biases, Ha), H=Ha, hd=hd)
    h3_s, _ = _mm_res_ln(o_att_s.reshape(Bd, D), w_aout, h2_s, ln_g[1, 0], ln_b[1, 0], alpha, name="attn_out_ln")

    ti_p, tw_p = _router(h3_p, w_router_p, n_experts)
    ti_s, tw_s = _router(h3_s, w_router_p, n_experts)
    tm_moe = 512
    src, gate, te, used, p0, p1 = _routing_tables(
        jnp.concatenate([ti_p, ti_s]), jnp.concatenate([tw_p, tw_s]), n_experts, tm_moe)
    h3_all = jnp.concatenate([h3_p, h3_s], axis=0)
    x_sorted = _gather_rows(h3_all, src)
    y_sorted = _ffn(x_sorted, w_moe_gu, w_moe_down, te, used, tm=tm_moe, tf=tf_exp, mode="gate",
                    gate=gate, name="ffn_moe")
    y_p = _combine_ln(y_sorted, p0[:N], p1[:N], h3_p, ln_g[1, 1], ln_b[1, 1], alpha)
    y_s = _combine_ln(y_sorted, p0[N:], p1[N:], h3_s, ln_g[1, 1], ln_b[1, 1], alpha)

    return (y_p.reshape(B, T, D), y_s.reshape(Bd, 1, D),
            state_p[None], state_s[None],
            k_p.reshape(B, T, Ha, hd), v_p.reshape(B, T, Ha, hd),
            k_s.reshape(Bd, 1, Ha, hd), v_s.reshape(Bd, 1, Ha, hd))
```

```python
import functools
import math

import jax
import jax.numpy as jnp
import numpy as np
from jax import lax
from jax.experimental import pallas as pl
from jax.experimental.pallas import tpu as pltpu

F32 = jnp.float32
BF16 = jnp.bfloat16

GLA_TAU = 16.0
GLA_CHUNK = 64
LN_EPS = 1e-5
NEG_INF = -1e30
DSW_GROUPS = ((128, 1), (512, 4), (2048, 16))
N_BUCKETS = 32
TOP_K = 2

LANES = 128
VMEM_LIMIT_BYTES = 56 * 1024 * 1024
ATTN_BLOCK = 128


def _params(*sem):
    return pltpu.CompilerParams(dimension_semantics=sem, vmem_limit_bytes=VMEM_LIMIT_BYTES)


def _layer_norm(x, g, b):
    mu = jnp.mean(x, axis=-1, keepdims=True)
    xc = x - mu
    var = jnp.mean(xc * xc, axis=-1, keepdims=True)
    return xc * lax.rsqrt(var + LN_EPS) * g + b


def _dot(a, b):
    return jnp.dot(a, b, preferred_element_type=F32)


def _dot_nt(a, b):
    return lax.dot_general(a, b, (((1,), (1,)), ((), ())), preferred_element_type=F32)


def _dot_tn(a, b):
    return lax.dot_general(a, b, (((0,), (0,)), ((), ())), preferred_element_type=F32)


def _dot_exact(a, b):
    return jnp.dot(a, b, preferred_element_type=F32, precision=lax.Precision.HIGHEST)


def _mm_kernel(a_ref, w_ref, o_ref, *, scale):
    acc = _dot(a_ref[...], w_ref[...])
    if scale != 1.0:
        acc = acc * scale
    o_ref[...] = acc.astype(o_ref.dtype)


def _mm(a, w, out_dtype=F32, scale=1.0, tm=1024, tn=1024, name="mm"):
    M, K = a.shape
    N = w.shape[1]
    tm = min(tm, M)
    tn = math.gcd(tn, N)
    assert M % tm == 0 and N % tn == 0, (M, N, tm, tn)
    return pl.pallas_call(
        functools.partial(_mm_kernel, scale=scale),
        grid=(M // tm, N // tn),
        in_specs=[pl.BlockSpec((tm, K), lambda i, j: (i, 0)),
                  pl.BlockSpec((K, tn), lambda i, j: (0, j))],
        out_specs=pl.BlockSpec((tm, tn), lambda i, j: (i, j)),
        out_shape=jax.ShapeDtypeStruct((M, N), out_dtype),
        compiler_params=_params("parallel", "arbitrary"),
        name=name,
    )(a, w)


def _mm_res_ln_kernel(a_ref, w_ref, res_ref, g_ref, b_ref, o_ref, obf_ref, *, alpha):
    mix = _dot(a_ref[...], w_ref[...])
    y = _layer_norm(alpha * res_ref[...] + mix, g_ref[...], b_ref[...])
    o_ref[...] = y
    obf_ref[...] = y.astype(BF16)


def _mm_res_ln(a, w, res, g, b, alpha, tm=512, name="mm_res_ln"):
    M, K = a.shape
    N = w.shape[1]
    tm = min(tm, M)
    assert M % tm == 0
    row = lambda i: (i, 0)
    fixed = lambda i: (0, 0)
    return pl.pallas_call(
        functools.partial(_mm_res_ln_kernel, alpha=alpha),
        grid=(M // tm,),
        in_specs=[pl.BlockSpec((tm, K), row), pl.BlockSpec((K, N), fixed),
                  pl.BlockSpec((tm, N), row), pl.BlockSpec((1, N), fixed), pl.BlockSpec((1, N), fixed)],
        out_specs=[pl.BlockSpec((tm, N), row), pl.BlockSpec((tm, N), row)],
        out_shape=[jax.ShapeDtypeStruct((M, N), F32), jax.ShapeDtypeStruct((M, N), BF16)],
        compiler_params=_params("parallel"),
        name=name,
    )(a, w, res, g.reshape(1, N), b.reshape(1, N))


def _ffn_kernel(te_ref, used_ref, x_ref, wg_ref, wu_ref, wd_ref, *rest, mode, alpha, nf):
    if mode == "ln":
        res_ref, g_ref, b_ref, o_ref, obf_ref = rest
    else:
        (o_ref,) = rest
    i = pl.program_id(0)
    f = pl.program_id(1)

    @pl.when(f == 0)
    def _():
        o_ref[...] = jnp.zeros_like(o_ref)

    @pl.when(used_ref[i] != 0)
    def _():
        x = x_ref[...].astype(BF16)
        g = _dot(x, wg_ref[0])
        u = _dot(x, wu_ref[0])
        act = (g * jax.nn.sigmoid(g) * u).astype(BF16)
        o_ref[...] += _dot(act, wd_ref[0])

    if mode == "ln":
        @pl.when(f == nf - 1)
        def _():
            y = _layer_norm(alpha * res_ref[...] + o_ref[...], g_ref[...], b_ref[...])
            o_ref[...] = y
            obf_ref[...] = y.astype(BF16)


def _ffn(x, w_gu, w_down, tile_expert, tile_used, *, tm, tf, mode, alpha=1.0,
         res=None, g=None, b=None, name="ffn"):
    M, D = x.shape
    E, F, _ = w_down.shape
    assert M % tm == 0 and F % tf == 0
    nf = F // tf
    row = lambda i, f, te, us: (i, 0)
    fixed = lambda i, f, te, us: (0, 0)
    in_specs = [
        pl.BlockSpec((tm, D), row),
        pl.BlockSpec((1, D, tf), lambda i, f, te, us: (te[i], 0, f * us[i])),
        pl.BlockSpec((1, D, tf), lambda i, f, te, us: (te[i], 0, nf + f * us[i])),
        pl.BlockSpec((1, tf, D), lambda i, f, te, us: (te[i], f * us[i], 0)),
    ]
    if mode == "ln":
        in_specs += [pl.BlockSpec((tm, D), row), pl.BlockSpec((1, D), fixed), pl.BlockSpec((1, D), fixed)]
        extra = (res, g.reshape(1, D), b.reshape(1, D))
        out_specs = [pl.BlockSpec((tm, D), row), pl.BlockSpec((tm, D), row)]
        out_shape = [jax.ShapeDtypeStruct((M, D), F32), jax.ShapeDtypeStruct((M, D), BF16)]
    else:
        extra = ()
        out_specs = pl.BlockSpec((tm, D), row)
        out_shape = jax.ShapeDtypeStruct((M, D), F32)
    return pl.pallas_call(
        functools.partial(_ffn_kernel, mode=mode, alpha=alpha, nf=nf),
        grid_spec=pltpu.PrefetchScalarGridSpec(
            num_scalar_prefetch=2, grid=(M // tm, nf), in_specs=in_specs, out_specs=out_specs),
        out_shape=out_shape,
        compiler_params=_params("parallel", "arbitrary"),
        name=name,
    )(tile_expert, tile_used, x, w_gu, w_gu, w_down, *extra)


def _log_sigmoid(z):
    return jnp.minimum(z, 0.0) - jnp.log(1.0 + jnp.exp(-jnp.abs(z)))


def _gla_prompt_kernel(q_ref, k_ref, v_ref, r_ref, alr_ref, wa_ref, ba_ref, ng_ref, o_ref, s_ref,
                       *, nchunk, scale):
    C = GLA_CHUNK
    dk = q_ref.shape[1]

    @pl.when(pl.program_id(2) == 0)
    def _():
        s_ref[...] = jnp.zeros_like(s_ref)

    rows = lax.broadcasted_iota(jnp.int32, (C, dk), 0)
    causal = lax.broadcasted_iota(jnp.int32, (C, C), 0) >= lax.broadcasted_iota(jnp.int32, (C, C), 1)
    for j in range(nchunk):
        sl = slice(j * C, (j + 1) * C)
        z = _dot_exact(alr_ref[sl, :], wa_ref[...]) + ba_ref[...]
        bcum = _log_sigmoid(z) / GLA_TAU
        shift = 1
        while shift < C:
            bcum = bcum + jnp.where(rows >= shift, pltpu.roll(bcum, shift, 0), 0.0)
            shift *= 2
        b_last = bcum[C - 1:C, :]
        kc = k_ref[sl, :]
        vc = v_ref[sl, :].astype(BF16)
        q_dec = (q_ref[sl, :] * scale * jnp.exp(bcum)).astype(BF16)
        k_dec = (kc * jnp.exp(-bcum)).astype(BF16)
        k_end = (kc * jnp.exp(b_last - bcum)).astype(BF16)
        a = jnp.where(causal, _dot_nt(q_dec, k_dec), 0.0)
        s_old = s_ref[0, 0]
        o = _dot(q_dec, s_old.astype(BF16)) + _dot(a.astype(BF16), vc)
        decay = jnp.transpose(jnp.broadcast_to(jnp.exp(b_last), (LANES, dk)))[:, 0:1]
        s_ref[0, 0] = s_old * decay + _dot_tn(k_end, vc)
        on = o * lax.rsqrt(jnp.mean(o * o, axis=-1, keepdims=True) + LN_EPS) * ng_ref[...]
        rr = r_ref[sl, :]
        o_ref[sl, :] = (on * (rr * jax.nn.sigmoid(rr))).astype(o_ref.dtype)


def _gla_prompt(qkvr, alr, w_a2p, b_a, norm_g, *, B, T, H, dk, dv, cb=256):
    N = B * T
    cb = min(cb, T)
    assert T % cb == 0 and cb % GLA_CHUNK == 0
    nc = T // cb
    dkt = H * dk
    dvt = H * dv
    assert (2 * dkt) % dv == 0
    v0 = (2 * dkt) // dv
    r0 = (2 * dkt + dvt) // dv
    rowblk = lambda b, h, c: b * nc + c
    return pl.pallas_call(
        functools.partial(_gla_prompt_kernel, nchunk=cb // GLA_CHUNK, scale=dk ** -0.5),
        grid=(B, H, nc),
        in_specs=[
            pl.BlockSpec((cb, dk), lambda b, h, c: (rowblk(b, h, c), h)),
            pl.BlockSpec((cb, dk), lambda b, h, c: (rowblk(b, h, c), H + h)),
            pl.BlockSpec((cb, dv), lambda b, h, c: (rowblk(b, h, c), v0 + h)),
            pl.BlockSpec((cb, dv), lambda b, h, c: (rowblk(b, h, c), r0 + h)),
            pl.BlockSpec((cb, LANES), lambda b, h, c: (rowblk(b, h, c), 0)),
            pl.BlockSpec((LANES, dk), lambda b, h, c: (0, h)),
            pl.BlockSpec((1, dk), lambda b, h, c: (0, h)),
            pl.BlockSpec((1, dv), lambda b, h, c: (0, 0)),
        ],
        out_specs=[
            pl.BlockSpec((cb, dv), lambda b, h, c: (rowblk(b, h, c), h)),
            pl.BlockSpec((1, 1, dk, dv), lambda b, h, c: (b, h, 0, 0)),
        ],
        out_shape=[jax.ShapeDtypeStruct((N, dvt), BF16), jax.ShapeDtypeStruct((B, H, dk, dv), F32)],
        compiler_params=_params("parallel", "parallel", "arbitrary"),
        name="gla_prompt",
    )(qkvr, qkvr, qkvr, qkvr, alr, w_a2p, b_a.reshape(1, dkt), norm_g.reshape(1, dv))


def _gla_sample_kernel(q_ref, k_ref, v_ref, r_ref, alr_ref, wat_ref, ba_ref, ng_ref, s0_ref, o_ref, s_ref,
                       *, scale):
    z = jnp.sum(wat_ref[...] * alr_ref[0], axis=-1, keepdims=True) + ba_ref[...]
    decay = jnp.exp(_log_sigmoid(z) / GLA_TAU)
    s_new = s0_ref[0, 0] * decay + k_ref[0, 0] * v_ref[0]
    s_ref[0, 0] = s_new
    o = jnp.sum((q_ref[0, 0] * scale) * s_new, axis=0, keepdims=True)
    on = o * lax.rsqrt(jnp.mean(o * o, axis=-1, keepdims=True) + LN_EPS) * ng_ref[...]
    rr = r_ref[0]
    o_ref[0] = (on * (rr * jax.nn.sigmoid(rr))).astype(o_ref.dtype)


def _gla_sample(qkvr, alr, w_a2tp, b_a, norm_g, s0, *, H, dk, dv):
    Bd = qkvr.shape[0]
    dkt = H * dk
    dvt = H * dv
    q = qkvr[:, :dkt].reshape(Bd, H, dk, 1)
    k = qkvr[:, dkt:2 * dkt].reshape(Bd, H, dk, 1)
    v = qkvr[:, 2 * dkt:2 * dkt + dvt].reshape(Bd, 1, dvt)
    r = qkvr[:, 2 * dkt + dvt:].reshape(Bd, 1, dvt)
    col = lambda b, h: (b, h, 0, 0)
    return pl.pallas_call(
        functools.partial(_gla_sample_kernel, scale=dk ** -0.5),
        grid=(Bd, H),
        in_specs=[
            pl.BlockSpec((1, 1, dk, 1), col),
            pl.BlockSpec((1, 1, dk, 1), col),
            pl.BlockSpec((1, 1, dv), lambda b, h: (b, 0, h)),
            pl.BlockSpec((1, 1, dv), lambda b, h: (b, 0, h)),
            pl.BlockSpec((1, 1, LANES), lambda b, h: (b, 0, 0)),
            pl.BlockSpec((dk, LANES), lambda b, h: (h, 0)),
            pl.BlockSpec((dk, 1), lambda b, h: (h, 0)),
            pl.BlockSpec((1, dv), lambda b, h: (0, 0)),
            pl.BlockSpec((1, 1, dk, dv), col),
        ],
        out_specs=[
            pl.BlockSpec((1, 1, dv), lambda b, h: (b, 0, h)),
            pl.BlockSpec((1, 1, dk, dv), col),
        ],
        out_shape=[jax.ShapeDtypeStruct((Bd, 1, dvt), BF16), jax.ShapeDtypeStruct((Bd, H, dk, dv), F32)],
        compiler_params=_params("parallel", "parallel"),
        name="gla_sample",
    )(q, k, v, r, alr.reshape(Bd, 1, LANES), w_a2tp, b_a.reshape(dkt, 1), norm_g.reshape(1, dv), s0)


def _attn_prompt_kernel(q0_ref, q1_ref, q2_ref, k_ref, v_ref, bvec_ref, o_ref, qb, kb, vb, m_s, l_s, acc_s,
                        *, T, dils):
    BQ = ATTN_BLOCK
    q_refs = (q0_ref, q1_ref, q2_ref)
    hd = k_ref.shape[2]
    nblk = T // BQ

    def rows_of(start, n, d):
        return pl.ds(start, n) if d == 1 else pl.ds(start, n, stride=d)

    def softmax_pv(s, v):
        m = jnp.max(s, axis=-1, keepdims=True)
        p = jnp.exp(s - m)
        l = jnp.sum(p, axis=-1, keepdims=True)
        pv = jnp.einsum("bqk,bkd->bqd", p.astype(BF16), v, preferred_element_type=F32)
        return m, l, pv

    for g, d in enumerate(dils):
        nb = T // (d * BQ)
        bias = pltpu.roll(jnp.broadcast_to(bvec_ref[g, 0], (BQ, 3 * BQ)), 0, 1, stride=1, stride_axis=0)
        blocks = [(r, 0) for r in range(d)] + [(r, i) for r in range(d) for i in range(1, nb)]
        nfirst = d
        for e, (r, i) in enumerate(blocks):
            qb[e] = q_refs[g][0, rows_of(r + i * BQ * d, BQ, d), :].astype(BF16)
            if i == 0:
                kb[e, 0:BQ, :] = k_ref[0, rows_of(r, BQ, d), :].astype(BF16)
                vb[e, 0:BQ, :] = v_ref[0, rows_of(r, BQ, d), :].astype(BF16)
            else:
                kb[e] = k_ref[0, rows_of(r + (i - 1) * BQ * d, 2 * BQ, d), :].astype(BF16)
                vb[e] = v_ref[0, rows_of(r + (i - 1) * BQ * d, 2 * BQ, d), :].astype(BF16)
        s_first = jnp.einsum("bqd,bkd->bqk", qb[0:nfirst], kb[0:nfirst, 0:BQ, :],
                             preferred_element_type=F32) + bias[:, BQ:2 * BQ]
        stats = [softmax_pv(s_first, vb[0:nfirst, 0:BQ, :])]
        if nb > 1:
            s_rest = jnp.einsum("bqd,bkd->bqk", qb[nfirst:nblk], kb[nfirst:nblk],
                                preferred_element_type=F32) + bias[:, 0:2 * BQ]
            stats.append(softmax_pv(s_rest, vb[nfirst:nblk]))
        for e, (r, i) in enumerate(blocks):
            m, l, pv = stats[0] if e < nfirst else stats[1]
            idx = e if e < nfirst else e - nfirst
            rows = rows_of(r + i * BQ * d, BQ, d)
            m_s[g, rows, :] = jnp.broadcast_to(m[idx], (BQ, hd))
            l_s[g, rows, :] = jnp.broadcast_to(l[idx], (BQ, hd))
            acc_s[g, rows, :] = pv[idx]

    CH = 2 * BQ
    G = len(dils)
    for c in range(T // CH):
        sl = slice(c * CH, (c + 1) * CH)
        ms = [m_s[g, sl, :] for g in range(G)]
        mm = functools.reduce(jnp.maximum, ms)
        ws = [jnp.exp(m - mm) for m in ms]
        num = functools.reduce(lambda a, b: a + b, [ws[g] * acc_s[g, sl, :] for g in range(G)])
        den = functools.reduce(lambda a, b: a + b, [ws[g] * l_s[g, sl, :] for g in range(G)])
        o_ref[0, sl, :] = (num / den).astype(o_ref.dtype)


def _attn_prompt(q, k, v, bias_vecs, *, B, T, H, hd):
    dils = tuple(d for _, d in DSW_GROUPS)
    G = len(dils)
    BQ = ATTN_BLOCK
    for w, d in DSW_GROUPS:
        assert w // d == BQ and T % (d * BQ) == 0
    assert hd == LANES
    nblk = T // BQ
    qspec = lambda g: pl.BlockSpec((1, T, hd), lambda b, h: (b, 0, g * H + h))
    kvspec = pl.BlockSpec((1, T, hd), lambda b, h: (b, 0, h))
    return pl.pallas_call(
        functools.partial(_attn_prompt_kernel, T=T, dils=dils),
        grid=(B, H),
        in_specs=[qspec(0), qspec(1), qspec(2), kvspec, kvspec,
                  pl.BlockSpec((G, 1, 1, 3 * BQ), lambda b, h: (0, h, 0, 0))],
        out_specs=pl.BlockSpec((1, T, hd), lambda b, h: (b, 0, h)),
        out_shape=jax.ShapeDtypeStruct((B, T, H * hd), BF16),
        scratch_shapes=[pltpu.VMEM((nblk, BQ, hd), BF16), pltpu.VMEM((nblk, 2 * BQ, hd), BF16),
                        pltpu.VMEM((nblk, 2 * BQ, hd), BF16),
                        pltpu.VMEM((G, T, hd), F32), pltpu.VMEM((G, T, hd), F32), pltpu.VMEM((G, T, hd), F32)],
        compiler_params=_params("parallel", "parallel"),
        name="attn_prompt",
    )(q, q, q, k, v, bias_vecs)


def _attn_sample_kernel(q_ref, kn_ref, vn_ref, ck0, ck1, ck2, cv0, cv1, cv2, bias_ref, o_ref, *, G):
    cks = (ck0, ck1, ck2)
    cvs = (cv0, cv1, cv2)
    BQ = ATTN_BLOCK
    kn = kn_ref[0]
    vn = vn_ref[0]
    s_cache, s_new = [], []
    for g in range(G):
        qg = q_ref[0, g]
        s_cache.append(jnp.sum(cks[g][0] * qg[None], axis=-1, keepdims=True) + bias_ref[g, 0:BQ])
        s_new.append(jnp.sum(kn * qg, axis=-1, keepdims=True) + bias_ref[g, BQ])
    mx = s_new[0]
    for g in range(G):
        mx = jnp.maximum(mx, jnp.maximum(s_new[g], jnp.max(s_cache[g], axis=0)))
    den = jnp.zeros_like(mx)
    o = jnp.zeros(kn.shape, F32)
    for g in range(G):
        pc = jnp.exp(s_cache[g] - mx)
        pn = jnp.exp(s_new[g] - mx)
        den = den + jnp.sum(pc, axis=0) + pn
        o = o + jnp.sum(pc * cvs[g][0], axis=0) + pn * vn
    o_ref[0] = (o / den).astype(o_ref.dtype)


def _attn_sample(q, k_new, v_new, cache_k, cache_v, bias_rows, *, H, hd):
    Bd, Lc = cache_k.shape[:2]
    G = len(DSW_GROUPS)
    BQ = ATTN_BLOCK
    views_k, views_v, specs = [], [], []
    for w, d in DSW_GROUPS:
        assert w // d == BQ and Lc % (d * BQ) == 0
        views_k.append(cache_k.reshape(Bd, Lc // d, d, H, hd))
        views_v.append(cache_v.reshape(Bd, Lc // d, d, H, hd))
        last = Lc // d // BQ - 1
        specs.append(pl.BlockSpec((1, BQ, None, H, hd),
                                  functools.partial(lambda b, last: (b, last, 0, 0, 0), last=last)))
    one = lambda b: (b, 0, 0)
    return pl.pallas_call(
        functools.partial(_attn_sample_kernel, G=G),
        grid=(Bd,),
        in_specs=[pl.BlockSpec((1, G, H, hd), lambda b: (b, 0, 0, 0)),
                  pl.BlockSpec((1, H, hd), one), pl.BlockSpec((1, H, hd), one),
                  *specs, *specs,
                  pl.BlockSpec(bias_rows.shape, lambda b: (0, 0, 0, 0))],
        out_specs=pl.BlockSpec((1, H, hd), one),
        out_shape=jax.ShapeDtypeStruct((Bd, H, hd), BF16),
        compiler_params=_params("parallel"),
        name="attn_sample",
    )(q.reshape(Bd, G, H, hd), k_new.reshape(Bd, H, hd), v_new.reshape(Bd, H, hd),
      *views_k, *views_v, bias_rows)


def _router_kernel(h_ref, w_ref, o_ref, *, n_experts):
    logits = _dot_exact(h_ref[...], w_ref[...])
    lane = lax.broadcasted_iota(jnp.int32, logits.shape, 1)
    lanef = lane.astype(F32)
    logits = jnp.where(lane < n_experts, logits, -jnp.inf)
    v1 = jnp.max(logits, axis=-1, keepdims=True)
    i1 = jnp.min(jnp.where(logits == v1, lanef, float(LANES)), axis=-1, keepdims=True)
    rest = jnp.where(lanef == i1, -jnp.inf, logits)
    v2 = jnp.max(rest, axis=-1, keepdims=True)
    i2 = jnp.min(jnp.where(rest == v2, lanef, float(LANES)), axis=-1, keepdims=True)
    e2 = jnp.exp(v2 - v1)
    w1 = 1.0 / (1.0 + e2)
    w2 = e2 / (1.0 + e2)
    o_ref[...] = jnp.where(lane == 0, i1, jnp.where(lane == 1, i2, jnp.where(lane == 2, w1, w2)))


def _router(h, w_router_p, n_experts, tm=512):
    M, D = h.shape
    tm = min(tm, M)
    assert M % tm == 0
    out = pl.pallas_call(
        functools.partial(_router_kernel, n_experts=n_experts),
        grid=(M // tm,),
        in_specs=[pl.BlockSpec((tm, D), lambda i: (i, 0)), pl.BlockSpec((D, LANES), lambda i: (0, 0))],
        out_specs=pl.BlockSpec((tm, LANES), lambda i: (i, 0)),
        out_shape=jax.ShapeDtypeStruct((M, LANES), F32),
        compiler_params=_params("parallel"),
        name="router",
    )(h, w_router_p)
    return out[:, 0:2].astype(jnp.int32), out[:, 2:4]


def _row_copy(src_ref, dst_ref, sem, src_row, dst_row):
    return pltpu.make_async_copy(src_ref.at[pl.ds(src_row, 1), :], dst_ref.at[pl.ds(dst_row, 1), :], sem)


DMA_UNROLL = 8


def _gather_kernel(idx_ref, src_ref, o_ref, sem, *, R):
    base = pl.program_id(0) * R

    def start(r, c):
        _row_copy(src_ref, o_ref, sem, idx_ref[base + r], r).start()
        return c

    lax.fori_loop(0, R, start, 0, unroll=math.gcd(DMA_UNROLL, R))

    def wait(r, c):
        _row_copy(src_ref, o_ref, sem, 0, r).wait()
        return c

    lax.fori_loop(0, R, wait, 0, unroll=math.gcd(DMA_UNROLL, R))


def _gather_rows(src, idx, R=512):
    M = idx.shape[0]
    D = src.shape[1]
    R = min(R, M)
    assert M % R == 0
    return pl.pallas_call(
        functools.partial(_gather_kernel, R=R),
        grid_spec=pltpu.PrefetchScalarGridSpec(
            num_scalar_prefetch=1, grid=(M // R,),
            in_specs=[pl.BlockSpec(memory_space=pl.ANY)],
            out_specs=pl.BlockSpec((R, D), lambda i, idx: (i, 0)),
            scratch_shapes=[pltpu.SemaphoreType.DMA(())]),
        out_shape=jax.ShapeDtypeStruct((M, D), src.dtype),
        compiler_params=_params("arbitrary"),
        name="moe_dispatch",
    )(idx, src)


def _combine_ln_kernel(p0_ref, p1_ref, rows_ref, w_ref, h_ref, g_ref, b_ref, o_ref, buf0, buf1, sem, *, R, alpha):
    base = pl.program_id(0) * R

    def start(r, c):
        _row_copy(rows_ref, buf0, sem, p0_ref[base + r], r).start()
        _row_copy(rows_ref, buf1, sem, p1_ref[base + r], r).start()
        return c

    lax.fori_loop(0, R, start, 0, unroll=math.gcd(DMA_UNROLL, R))

    def wait(r, c):
        _row_copy(rows_ref, buf0, sem, 0, r).wait()
        _row_copy(rows_ref, buf1, sem, 0, r).wait()
        return c

    lax.fori_loop(0, R, wait, 0, unroll=math.gcd(DMA_UNROLL, R))
    w = w_ref[...]
    moe = w[:, 0:1] * buf0[...] + w[:, 1:2] * buf1[...]
    o_ref[...] = _layer_norm(alpha * h_ref[...] + moe, g_ref[...], b_ref[...])


def _combine_ln(rows, p0, p1, w, h, g, b, alpha, R=256):
    M, D = h.shape
    R = min(R, M)
    assert M % R == 0
    row = lambda i, p0, p1: (i, 0)
    fixed = lambda i, p0, p1: (0, 0)
    return pl.pallas_call(
        functools.partial(_combine_ln_kernel, R=R, alpha=alpha),
        grid_spec=pltpu.PrefetchScalarGridSpec(
            num_scalar_prefetch=2, grid=(M // R,),
            in_specs=[pl.BlockSpec(memory_space=pl.ANY), pl.BlockSpec((R, TOP_K), row), pl.BlockSpec((R, D), row),
                      pl.BlockSpec((1, D), fixed), pl.BlockSpec((1, D), fixed)],
            out_specs=pl.BlockSpec((R, D), row),
            scratch_shapes=[pltpu.VMEM((R, D), F32), pltpu.VMEM((R, D), F32), pltpu.SemaphoreType.DMA(())]),
        out_shape=jax.ShapeDtypeStruct((M, D), F32),
        compiler_params=_params("arbitrary"),
        name="moe_combine_ln",
    )(p0, p1, rows, w, h, g.reshape(1, D), b.reshape(1, D))


def _routing_tables(top_i, n_experts, tm):
    nt = top_i.shape[0]
    na = TOP_K * nt
    e = top_i.reshape(na)
    onehot = (e[None, :] == jnp.arange(n_experts, dtype=jnp.int32)[:, None]).astype(jnp.int32)
    csum = jnp.cumsum(onehot, axis=1)
    counts = csum[:, -1]
    padded = ((counts + tm - 1) // tm) * tm
    pend = jnp.cumsum(padded)
    pstart = pend - padded
    slot = jnp.sum(onehot * (csum - 1 + pstart[:, None]), axis=0)
    ntiles = (na + n_experts * (tm - 1) + tm - 1) // tm
    mpad = ntiles * tm
    src = jnp.zeros((mpad,), jnp.int32).at[slot].set(
        jnp.arange(na, dtype=jnp.int32) // TOP_K, unique_indices=True, indices_are_sorted=False)
    tile_start = jnp.arange(ntiles, dtype=jnp.int32) * tm
    used = (tile_start < pend[-1]).astype(jnp.int32)
    te = jnp.minimum(jnp.sum((tile_start[:, None] >= pend[None, :]).astype(jnp.int32), axis=1), n_experts - 1)
    last_used = jnp.max(jnp.where(used > 0, te, 0))
    te = jnp.where(used > 0, te, last_used)
    slot2 = slot.reshape(nt, TOP_K)
    return src, te, used, slot2[:, 0], slot2[:, 1]


def _t5_bucket_ids(dist, max_dist):
    max_exact = N_BUCKETS // 2
    d = np.asarray(dist, dtype=np.int64)
    log_ratio = np.log(np.maximum(d, max_exact) / max_exact) / math.log(max_dist / max_exact)
    large = np.minimum(max_exact + (log_ratio * (N_BUCKETS - max_exact)).astype(np.int64), N_BUCKETS - 1)
    return np.where(d < max_exact, d, large).astype(np.int32)


def _group_biases(rel_bias, H):
    max_dist = max(w for w, _ in DSW_GROUPS)
    out = []
    for g, (w, d) in enumerate(DSW_GROUPS):
        buckets = _t5_bucket_ids(d * np.arange(w // d + 1), max_dist)
        out.append(rel_bias[buckets][:, g * H:(g + 1) * H].astype(F32))
    return out


def _prompt_bias_vecs(biases):
    BQ = ATTN_BLOCK
    vecs = []
    for bg in biases:
        v = jnp.concatenate([bg[::-1], jnp.full((2 * BQ - 1, bg.shape[1]), NEG_INF, F32)], axis=0)
        vecs.append(jnp.transpose(v)[:, None, :])
    return jnp.stack(vecs)


def _sample_bias_rows(biases):
    BQ = ATTN_BLOCK
    rows = []
    for bg in biases:
        rows.append(jnp.concatenate([bg[::-1][:BQ], bg[0:1], jnp.zeros((7, bg.shape[1]), F32)], axis=0))
    return jnp.stack(rows)[..., None]


def kernel(x_prompt, x_sample, state_gla, cache_k, cache_v, ln_g, ln_b, gla_w_in, gla_w_a2, gla_b_a,
           gla_norm_g, gla_w_out, kv_w, dsw_w_q, dsw_w_out, rel_bias, ffn_w_gu, ffn_w_down,
           moe_w_router, moe_w_gu, moe_w_down):
    B, T, D = x_prompt.shape
    Bd = x_sample.shape[0]
    depth = ln_g.shape[0]
    assert depth == 2 and x_sample.shape[1] == 1
    alpha = (2 * depth) ** 0.25
    rank, dkt = gla_w_a2.shape[1:]
    dv = gla_norm_g.shape[1]
    dvt = gla_w_out.shape[1]
    Hg = dvt // dv
    dk = dkt // Hg
    G = len(DSW_GROUPS)
    Ha = rel_bias.shape[1] // G
    hd = D // Ha
    n_experts = moe_w_router.shape[2]
    f_dense = ffn_w_down.shape[1]
    f_exp = moe_w_down.shape[2]
    N = B * T

    w_in = gla_w_in[0]
    c_a = 2 * dkt + dvt
    w_qkvr = jnp.concatenate([w_in[:, :c_a], w_in[:, c_a + rank:]], axis=1).astype(BF16)
    w_alr = jnp.pad(w_in[:, c_a:c_a + rank], ((0, 0), (0, LANES - rank))).astype(BF16)
    w_a2p = jnp.pad(gla_w_a2[0], ((0, LANES - rank), (0, 0)))
    w_a2tp = jnp.pad(gla_w_a2[0].T, ((0, 0), (0, LANES - rank)))
    w_gout = gla_w_out[0].astype(BF16)
    w_k = kv_w[:, :D].astype(BF16)
    w_v = kv_w[:, D:].astype(BF16)
    w_q = dsw_w_q[0].astype(BF16)
    w_aout = dsw_w_out[0].astype(BF16)
    w_ffn_gu = ffn_w_gu.astype(BF16)
    w_ffn_down = ffn_w_down.astype(BF16)
    w_moe_gu = moe_w_gu[0].astype(BF16)
    w_moe_down = moe_w_down[0].astype(BF16)
    w_router_p = jnp.pad(moe_w_router[0], ((0, 0), (0, LANES - n_experts)))
    biases = _group_biases(rel_bias, Ha)
    tf_dense = math.gcd(f_dense, 512)
    tf_exp = math.gcd(f_exp, 512)

    def dense_tables(m, tm):
        n = m // tm
        return jnp.zeros((n,), jnp.int32), jnp.ones((n,), jnp.int32)

    def layer0_in(x2):
        xb = x2.astype(BF16)
        return _mm(xb, w_qkvr, name="gla_in_proj"), _mm(xb, w_alr, name="gla_gate_proj")

    def layer0_out(o_gla, x2):
        m = x2.shape[0]
        h1, h1b = _mm_res_ln(o_gla, w_gout, x2, ln_g[0, 0], ln_b[0, 0], alpha, name="gla_out_ln")
        tm = min(512, m)
        te, us = dense_tables(m, tm)
        h2, h2b = _ffn(h1b, w_ffn_gu, w_ffn_down, te, us, tm=tm, tf=tf_dense, mode="ln", alpha=alpha,
                       res=h1, g=ln_g[0, 1], b=ln_b[0, 1], name="ffn_dense")
        k = _mm(h2b, w_k, name="k_proj")
        v = _mm(h2b, w_v, name="v_proj")
        q = _mm(h2b, w_q, scale=hd ** -0.5, name="q_proj")
        return h2, k, v, q

    xp = x_prompt.reshape(N, D)
    qkvr_p, alr_p = layer0_in(xp)
    o_gla_p, state_p = _gla_prompt(qkvr_p, alr_p, w_a2p, gla_b_a[0], gla_norm_g[0],
                                   B=B, T=T, H=Hg, dk=dk, dv=dv)
    h2_p, k_p, v_p, q_p = layer0_out(o_gla_p, xp)
    o_att_p = _attn_prompt(q_p.reshape(B, T, G * D), k_p.reshape(B, T, D), v_p.reshape(B, T, D),
                           _prompt_bias_vecs(biases), B=B, T=T, H=Ha, hd=hd)
    h3_p, _ = _mm_res_ln(o_att_p.reshape(N, D), w_aout, h2_p, ln_g[1, 0], ln_b[1, 0], alpha, name="attn_out_ln")

    xs = x_sample.reshape(Bd, D)
    qkvr_s, alr_s = layer0_in(xs)
    o_gla_s, state_s = _gla_sample(qkvr_s, alr_s, w_a2tp, gla_b_a[0], gla_norm_g[0], state_gla[0],
                                   H=Hg, dk=dk, dv=dv)
    h2_s, k_s, v_s, q_s = layer0_out(o_gla_s.reshape(Bd, dvt), xs)
    o_att_s = _attn_sample(q_s, k_s, v_s, cache_k, cache_v, _sample_bias_rows(biases), H=Ha, hd=hd)
    h3_s, _ = _mm_res_ln(o_att_s.reshape(Bd, D), w_aout, h2_s, ln_g[1, 0], ln_b[1, 0], alpha, name="attn_out_ln")

    ti_p, tw_p = _router(h3_p, w_router_p, n_experts)
    ti_s, tw_s = _router(h3_s, w_router_p, n_experts)
    tm_moe = 512
    src, te, used, p0, p1 = _routing_tables(jnp.concatenate([ti_p, ti_s]), n_experts, tm_moe)
    h3_all = jnp.concatenate([h3_p, h3_s], axis=0)
    x_sorted = _gather_rows(h3_all, src)
    y_sorted = _ffn(x_sorted, w_moe_gu, w_moe_down, te, used, tm=tm_moe, tf=tf_exp, mode="plain", name="ffn_moe")
    y_p = _combine_ln(y_sorted, p0[:N], p1[:N], tw_p, h3_p, ln_g[1, 1], ln_b[1, 1], alpha)
    y_s = _combine_ln(y_sorted, p0[N:], p1[N:], tw_s, h3_s, ln_g[1, 1], ln_b[1, 1], alpha)

    return (y_p.reshape(B, T, D), y_s.reshape(Bd, 1, D),
            state_p[None], state_s[None],
            k_p.reshape(B, T, Ha, hd), v_p.reshape(B, T, Ha, hd),
            k_s.reshape(Bd, 1, Ha, hd), v_s.reshape(Bd, 1, Ha, hd))
```

```python
import functools
import math

import jax
import jax.numpy as jnp
import numpy as np
from jax import lax
from jax.experimental import pallas as pl
from jax.experimental.pallas import tpu as pltpu

F32 = jnp.float32
BF16 = jnp.bfloat16

GLA_TAU = 16.0
GLA_CHUNK = 64
LN_EPS = 1e-5
NEG_INF = -1e30
DSW_GROUPS = ((128, 1), (512, 4), (2048, 16))
N_BUCKETS = 32
TOP_K = 2

LANES = 128
VMEM_LIMIT_BYTES = 56 * 1024 * 1024
ATTN_BLOCK = 128


def _params(*sem):
    return pltpu.CompilerParams(dimension_semantics=sem, vmem_limit_bytes=VMEM_LIMIT_BYTES)


def _layer_norm(x, g, b):
    mu = jnp.mean(x, axis=-1, keepdims=True)
    xc = x - mu
    var = jnp.mean(xc * xc, axis=-1, keepdims=True)
    return xc * lax.rsqrt(var + LN_EPS) * g + b


def _dot(a, b):
    return jnp.dot(a, b, preferred_element_type=F32)


def _dot_nt(a, b):
    return lax.dot_general(a, b, (((1,), (1,)), ((), ())), preferred_element_type=F32)


def _dot_tn(a, b):
    return lax.dot_general(a, b, (((0,), (0,)), ((), ())), preferred_element_type=F32)


def _dot_exact(a, b):
    return jnp.dot(a, b, preferred_element_type=F32, precision=lax.Precision.HIGHEST)


def _dot_hi_lo(a, b):
    a_hi = a.astype(BF16)
    b_hi = b.astype(BF16)
    a_lo = (a - a_hi.astype(F32)).astype(BF16)
    b_lo = (b - b_hi.astype(F32)).astype(BF16)
    return _dot(a_hi, b_hi) + (_dot(a_lo, b_hi) + _dot(a_hi, b_lo))


def _mm_kernel(a_ref, w_ref, *o_refs, scale, natural, by_head):
    acc = _dot(a_ref[...].astype(BF16), w_ref[...])
    if scale != 1.0:
        acc = acc * scale
    o_refs = list(o_refs)
    if natural:
        o_refs.pop(0)[...] = acc
    if by_head:
        o_head = o_refs.pop(0)
        for h in range(acc.shape[1] // LANES):
            o_head[h] = acc[:, h * LANES:(h + 1) * LANES]


def _mm(a, w, scale=1.0, tm=1024, tn=1024, natural=True, by_head=False, name="mm"):
    M, K = a.shape
    N = w.shape[1]
    tm = min(tm, M)
    tn = math.gcd(tn, N)
    assert M % tm == 0 and N % tn == 0 and tn % LANES == 0, (M, N, tm, tn)
    out_specs, out_shape = [], []
    if natural:
        out_specs.append(pl.BlockSpec((tm, tn), lambda i, j: (i, j)))
        out_shape.append(jax.ShapeDtypeStruct((M, N), F32))
    if by_head:
        out_specs.append(pl.BlockSpec((tn // LANES, tm, LANES), lambda i, j: (j, i, 0)))
        out_shape.append(jax.ShapeDtypeStruct((N // LANES, M, LANES), F32))
    outs = pl.pallas_call(
        functools.partial(_mm_kernel, scale=scale, natural=natural, by_head=by_head),
        grid=(M // tm, N // tn),
        in_specs=[pl.BlockSpec((tm, K), lambda i, j: (i, 0)),
                  pl.BlockSpec((K, tn), lambda i, j: (0, j))],
        out_specs=out_specs,
        out_shape=out_shape,
        compiler_params=_params("parallel", "arbitrary"),
        name=name,
    )(a, w)
    return outs[0] if len(outs) == 1 else outs


def _mm_res_ln_kernel(a_ref, w_ref, res_ref, g_ref, b_ref, o_ref, obf_ref, *, alpha):
    mix = _dot(a_ref[...], w_ref[...])
    y = _layer_norm(alpha * res_ref[...] + mix, g_ref[...], b_ref[...])
    o_ref[...] = y
    obf_ref[...] = y.astype(BF16)


def _mm_res_ln(a, w, res, g, b, alpha, tm=512, name="mm_res_ln"):
    M, K = a.shape
    N = w.shape[1]
    tm = min(tm, M)
    assert M % tm == 0
    row = lambda i: (i, 0)
    fixed = lambda i: (0, 0)
    return pl.pallas_call(
        functools.partial(_mm_res_ln_kernel, alpha=alpha),
        grid=(M // tm,),
        in_specs=[pl.BlockSpec((tm, K), row), pl.BlockSpec((K, N), fixed),
                  pl.BlockSpec((tm, N), row), pl.BlockSpec((1, N), fixed), pl.BlockSpec((1, N), fixed)],
        out_specs=[pl.BlockSpec((tm, N), row), pl.BlockSpec((tm, N), row)],
        out_shape=[jax.ShapeDtypeStruct((M, N), F32), jax.ShapeDtypeStruct((M, N), BF16)],
        compiler_params=_params("parallel"),
        name=name,
    )(a, w, res, g.reshape(1, N), b.reshape(1, N))


def _ffn_kernel(te_ref, used_ref, x_ref, wg_ref, wu_ref, wd_ref, *rest, mode, alpha, nf):
    if mode == "ln":
        res_ref, g_ref, b_ref, o_ref, obf_ref = rest
    else:
        (o_ref,) = rest
    i = pl.program_id(0)
    f = pl.program_id(1)

    @pl.when(f == 0)
    def _():
        o_ref[...] = jnp.zeros_like(o_ref)

    @pl.when(used_ref[i] != 0)
    def _():
        x = x_ref[...].astype(BF16)
        g = _dot(x, wg_ref[0])
        u = _dot(x, wu_ref[0])
        act = (g * jax.nn.sigmoid(g) * u).astype(BF16)
        o_ref[...] += _dot(act, wd_ref[0])

    if mode == "ln":
        @pl.when(f == nf - 1)
        def _():
            y = _layer_norm(alpha * res_ref[...] + o_ref[...], g_ref[...], b_ref[...])
            o_ref[...] = y
            obf_ref[...] = y.astype(BF16)


def _ffn(x, w_gu, w_down, tile_expert, tile_used, *, tm, tf, mode, alpha=1.0,
         res=None, g=None, b=None, name="ffn"):
    M, D = x.shape
    E, F, _ = w_down.shape
    assert M % tm == 0 and F % tf == 0
    nf = F // tf
    row = lambda i, f, te, us: (i, 0)
    fixed = lambda i, f, te, us: (0, 0)
    in_specs = [
        pl.BlockSpec((tm, D), row),
        pl.BlockSpec((1, D, tf), lambda i, f, te, us: (te[i], 0, f * us[i])),
        pl.BlockSpec((1, D, tf), lambda i, f, te, us: (te[i], 0, nf + f * us[i])),
        pl.BlockSpec((1, tf, D), lambda i, f, te, us: (te[i], f * us[i], 0)),
    ]
    if mode == "ln":
        in_specs += [pl.BlockSpec((tm, D), row), pl.BlockSpec((1, D), fixed), pl.BlockSpec((1, D), fixed)]
        extra = (res, g.reshape(1, D), b.reshape(1, D))
        out_specs = [pl.BlockSpec((tm, D), row), pl.BlockSpec((tm, D), row)]
        out_shape = [jax.ShapeDtypeStruct((M, D), F32), jax.ShapeDtypeStruct((M, D), BF16)]
    else:
        extra = ()
        out_specs = pl.BlockSpec((tm, D), row)
        out_shape = jax.ShapeDtypeStruct((M, D), F32)
    return pl.pallas_call(
        functools.partial(_ffn_kernel, mode=mode, alpha=alpha, nf=nf),
        grid_spec=pltpu.PrefetchScalarGridSpec(
            num_scalar_prefetch=2, grid=(M // tm, nf), in_specs=in_specs, out_specs=out_specs),
        out_shape=out_shape,
        compiler_params=_params("parallel", "arbitrary"),
        name=name,
    )(tile_expert, tile_used, x, w_gu, w_gu, w_down, *extra)


def _log_sigmoid(z):
    return jnp.minimum(z, 0.0) - jnp.log(1.0 + jnp.exp(-jnp.abs(z)))


def _gla_prompt_kernel(q_ref, k_ref, v_ref, r_ref, alr_ref, wa_ref, ba_ref, ng_ref, o_ref, s_ref,
                       *, nchunk, scale):
    C = GLA_CHUNK
    dk = q_ref.shape[1]

    @pl.when(pl.program_id(2) == 0)
    def _():
        s_ref[...] = jnp.zeros_like(s_ref)

    cb = nchunk * C
    dv = v_ref.shape[1]
    z = _dot_hi_lo(alr_ref[...], wa_ref[...]) + ba_ref[...]
    bcum = _log_sigmoid(z) / GLA_TAU
    row_in_chunk = lax.broadcasted_iota(jnp.int32, (cb, dk), 0) % C
    shift = 1
    while shift < C:
        bcum = bcum + jnp.where(row_in_chunk >= shift, pltpu.roll(bcum, shift, 0), 0.0)
        shift *= 2
    bc3 = bcum.reshape(nchunk, C, dk)
    b_last = bc3[:, C - 1:C, :]
    k3 = k_ref[...].reshape(nchunk, C, dk)
    v3 = v_ref[...].astype(BF16).reshape(nchunk, C, dv)
    q_dec = (q_ref[...].reshape(nchunk, C, dk) * scale * jnp.exp(bc3)).astype(BF16)
    k_dec = (k3 * jnp.exp(-bc3)).astype(BF16)
    k_end = (k3 * jnp.exp(b_last - bc3)).astype(BF16)
    causal = lax.broadcasted_iota(jnp.int32, (C, C), 0) >= lax.broadcasted_iota(jnp.int32, (C, C), 1)
    a = jnp.where(causal, jnp.einsum("nid,njd->nij", q_dec, k_dec, preferred_element_type=F32), 0.0)
    o = jnp.einsum("nij,nje->nie", a.astype(BF16), v3, preferred_element_type=F32)
    decay_rows = jnp.concatenate([jnp.exp(b_last).reshape(nchunk, dk), jnp.ones((LANES - nchunk, dk), F32)], axis=0)
    decay_cols = jnp.transpose(decay_rows)
    s = s_ref[0, 0]
    s_starts = []
    for j in range(nchunk):
        s_starts.append(s.astype(BF16))
        s = s * decay_cols[:, j:j + 1] + _dot_tn(k_end[j], v3[j])
    s_ref[0, 0] = s
    o = o + jnp.einsum("nid,nde->nie", q_dec, jnp.stack(s_starts), preferred_element_type=F32)
    o = o.reshape(cb, dv)
    on = o * lax.rsqrt(jnp.mean(o * o, axis=-1, keepdims=True) + LN_EPS) * ng_ref[...]
    rr = r_ref[...]
    o_ref[...] = (on * (rr * jax.nn.sigmoid(rr))).astype(o_ref.dtype)


def _gla_prompt(proj, w_a2p, b_a, norm_g, *, B, T, H, dk, dv, cb=512):
    N = B * T
    cb = min(cb, T)
    assert T % cb == 0 and cb % GLA_CHUNK == 0
    nc = T // cb
    dkt = H * dk
    dvt = H * dv
    assert (2 * dkt) % dv == 0 and (2 * dkt + 2 * dvt) % LANES == 0
    v0 = (2 * dkt) // dv
    r0 = (2 * dkt + dvt) // dv
    a0 = (2 * dkt + 2 * dvt) // LANES
    rowblk = lambda b, h, c: b * nc + c
    return pl.pallas_call(
        functools.partial(_gla_prompt_kernel, nchunk=cb // GLA_CHUNK, scale=dk ** -0.5),
        grid=(B, H, nc),
        in_specs=[
            pl.BlockSpec((cb, dk), lambda b, h, c: (rowblk(b, h, c), h)),
            pl.BlockSpec((cb, dk), lambda b, h, c: (rowblk(b, h, c), H + h)),
            pl.BlockSpec((cb, dv), lambda b, h, c: (rowblk(b, h, c), v0 + h)),
            pl.BlockSpec((cb, dv), lambda b, h, c: (rowblk(b, h, c), r0 + h)),
            pl.BlockSpec((cb, LANES), lambda b, h, c: (rowblk(b, h, c), a0)),
            pl.BlockSpec((LANES, dk), lambda b, h, c: (0, h)),
            pl.BlockSpec((1, dk), lambda b, h, c: (0, h)),
            pl.BlockSpec((1, dv), lambda b, h, c: (0, 0)),
        ],
        out_specs=[
            pl.BlockSpec((cb, dv), lambda b, h, c: (rowblk(b, h, c), h)),
            pl.BlockSpec((1, 1, dk, dv), lambda b, h, c: (b, h, 0, 0)),
        ],
        out_shape=[jax.ShapeDtypeStruct((N, dvt), BF16), jax.ShapeDtypeStruct((B, H, dk, dv), F32)],
        compiler_params=_params("parallel", "parallel", "arbitrary"),
        name="gla_prompt",
    )(proj, proj, proj, proj, proj, w_a2p, b_a.reshape(1, dkt), norm_g.reshape(1, dv))


def _gla_sample_kernel(q_ref, k_ref, v_ref, r_ref, alr_ref, wat_ref, ba_ref, ng_ref, s0_ref, o_ref, s_ref,
                       *, scale):
    z = jnp.sum(wat_ref[...] * alr_ref[0], axis=-1, keepdims=True) + ba_ref[...]
    decay = jnp.exp(_log_sigmoid(z) / GLA_TAU)
    s_new = s0_ref[0, 0] * decay + k_ref[0, 0] * v_ref[0]
    s_ref[0, 0] = s_new
    o = jnp.sum((q_ref[0, 0] * scale) * s_new, axis=0, keepdims=True)
    on = o * lax.rsqrt(jnp.mean(o * o, axis=-1, keepdims=True) + LN_EPS) * ng_ref[...]
    rr = r_ref[0]
    o_ref[0] = (on * (rr * jax.nn.sigmoid(rr))).astype(o_ref.dtype)


def _gla_sample(qkvr, alr, w_a2tp, b_a, norm_g, s0, *, H, dk, dv):
    Bd = qkvr.shape[0]
    dkt = H * dk
    dvt = H * dv
    q = qkvr[:, :dkt].reshape(Bd, H, dk, 1)
    k = qkvr[:, dkt:2 * dkt].reshape(Bd, H, dk, 1)
    v = qkvr[:, 2 * dkt:2 * dkt + dvt].reshape(Bd, 1, dvt)
    r = qkvr[:, 2 * dkt + dvt:].reshape(Bd, 1, dvt)
    col = lambda b, h: (b, h, 0, 0)
    return pl.pallas_call(
        functools.partial(_gla_sample_kernel, scale=dk ** -0.5),
        grid=(Bd, H),
        in_specs=[
            pl.BlockSpec((1, 1, dk, 1), col),
            pl.BlockSpec((1, 1, dk, 1), col),
            pl.BlockSpec((1, 1, dv), lambda b, h: (b, 0, h)),
            pl.BlockSpec((1, 1, dv), lambda b, h: (b, 0, h)),
            pl.BlockSpec((1, 1, LANES), lambda b, h: (b, 0, 0)),
            pl.BlockSpec((dk, LANES), lambda b, h: (h, 0)),
            pl.BlockSpec((dk, 1), lambda b, h: (h, 0)),
            pl.BlockSpec((1, dv), lambda b, h: (0, 0)),
            pl.BlockSpec((1, 1, dk, dv), col),
        ],
        out_specs=[
            pl.BlockSpec((1, 1, dv), lambda b, h: (b, 0, h)),
            pl.BlockSpec((1, 1, dk, dv), col),
        ],
        out_shape=[jax.ShapeDtypeStruct((Bd, 1, dvt), BF16), jax.ShapeDtypeStruct((Bd, H, dk, dv), F32)],
        compiler_params=_params("parallel", "parallel"),
        name="gla_sample",
    )(q, k, v, r, alr.reshape(Bd, 1, LANES), w_a2tp, b_a.reshape(dkt, 1), norm_g.reshape(1, dv), s0)


def _attn_prompt_kernel(q0_ref, q1_ref, q2_ref, k_ref, v_ref, bvec_ref, o_ref, qb, kb, vb, m_s, l_s, acc_s,
                        *, T, dils):
    BQ = ATTN_BLOCK
    q_refs = (q0_ref, q1_ref, q2_ref)
    hd = k_ref.shape[2]
    nblk = T // BQ

    def rows_of(start, n, d):
        return pl.ds(start, n) if d == 1 else pl.ds(start, n, stride=d)

    def softmax_pv(s, v):
        m = jnp.max(s, axis=-1, keepdims=True)
        p = jnp.exp(s - m)
        l = jnp.sum(p, axis=-1, keepdims=True)
        pv = jnp.einsum("bqk,bkd->bqd", p.astype(BF16), v, preferred_element_type=F32)
        return m, l, pv

    for g, d in enumerate(dils):
        nb = T // (d * BQ)
        bias = pltpu.roll(jnp.broadcast_to(bvec_ref[g, 0], (BQ, 3 * BQ)), 0, 1, stride=1, stride_axis=0)
        blocks = [(r, 0) for r in range(d)] + [(r, i) for r in range(d) for i in range(1, nb)]
        nfirst = d
        for e, (r, i) in enumerate(blocks):
            qb[e] = q_refs[g][0, rows_of(r + i * BQ * d, BQ, d), :].astype(BF16)
            if i == 0:
                kb[e, 0:BQ, :] = k_ref[0, rows_of(r, BQ, d), :].astype(BF16)
                vb[e, 0:BQ, :] = v_ref[0, rows_of(r, BQ, d), :].astype(BF16)
            else:
                kb[e] = k_ref[0, rows_of(r + (i - 1) * BQ * d, 2 * BQ, d), :].astype(BF16)
                vb[e] = v_ref[0, rows_of(r + (i - 1) * BQ * d, 2 * BQ, d), :].astype(BF16)
        s_first = jnp.einsum("bqd,bkd->bqk", qb[0:nfirst], kb[0:nfirst, 0:BQ, :],
                             preferred_element_type=F32) + bias[:, BQ:2 * BQ]
        stats = [softmax_pv(s_first, vb[0:nfirst, 0:BQ, :])]
        if nb > 1:
            s_rest = jnp.einsum("bqd,bkd->bqk", qb[nfirst:nblk], kb[nfirst:nblk],
                                preferred_element_type=F32) + bias[:, 0:2 * BQ]
            stats.append(softmax_pv(s_rest, vb[nfirst:nblk]))
        for e, (r, i) in enumerate(blocks):
            m, l, pv = stats[0] if e < nfirst else stats[1]
            idx = e if e < nfirst else e - nfirst
            rows = rows_of(r + i * BQ * d, BQ, d)
            m_s[g, rows, :] = jnp.broadcast_to(m[idx], (BQ, hd))
            l_s[g, rows, :] = jnp.broadcast_to(l[idx], (BQ, hd))
            acc_s[g, rows, :] = pv[idx]

    CH = 2 * BQ
    G = len(dils)
    for c in range(T // CH):
        sl = slice(c * CH, (c + 1) * CH)
        ms = [m_s[g, sl, :] for g in range(G)]
        mm = functools.reduce(jnp.maximum, ms)
        ws = [jnp.exp(m - mm) for m in ms]
        num = functools.reduce(lambda a, b: a + b, [ws[g] * acc_s[g, sl, :] for g in range(G)])
        den = functools.reduce(lambda a, b: a + b, [ws[g] * l_s[g, sl, :] for g in range(G)])
        o_ref[0, sl, :] = (num / den).astype(o_ref.dtype)


def _attn_prompt(q, k, v, bias_vecs, *, B, T, H, hd):
    dils = tuple(d for _, d in DSW_GROUPS)
    G = len(dils)
    BQ = ATTN_BLOCK
    for w, d in DSW_GROUPS:
        assert w // d == BQ and T % (d * BQ) == 0
    assert hd == LANES
    nblk = T // BQ
    qspec = lambda g: pl.BlockSpec((1, T, hd), lambda b, h: (g * H + h, b, 0))
    kvspec = pl.BlockSpec((1, T, hd), lambda b, h: (h, b, 0))
    return pl.pallas_call(
        functools.partial(_attn_prompt_kernel, T=T, dils=dils),
        grid=(B, H),
        in_specs=[qspec(0), qspec(1), qspec(2), kvspec, kvspec,
                  pl.BlockSpec((G, 1, 1, 3 * BQ), lambda b, h: (0, h, 0, 0))],
        out_specs=pl.BlockSpec((1, T, hd), lambda b, h: (b, 0, h)),
        out_shape=jax.ShapeDtypeStruct((B, T, H * hd), BF16),
        scratch_shapes=[pltpu.VMEM((nblk, BQ, hd), BF16), pltpu.VMEM((nblk, 2 * BQ, hd), BF16),
                        pltpu.VMEM((nblk, 2 * BQ, hd), BF16),
                        pltpu.VMEM((G, T, hd), F32), pltpu.VMEM((G, T, hd), F32), pltpu.VMEM((G, T, hd), F32)],
        compiler_params=_params("parallel", "parallel"),
        name="attn_prompt",
    )(q, q, q, k, v, bias_vecs)


def _attn_sample_kernel(q_ref, kn_ref, vn_ref, ck0, ck1, ck2, cv0, cv1, cv2, bias_ref, o_ref, *, G):
    cks = (ck0, ck1, ck2)
    cvs = (cv0, cv1, cv2)
    BQ = ATTN_BLOCK
    kn = kn_ref[0]
    vn = vn_ref[0]
    s_cache, s_new = [], []
    for g in range(G):
        qg = q_ref[0, g]
        s_cache.append(jnp.sum(cks[g][0] * qg[None], axis=-1, keepdims=True) + bias_ref[g, 0:BQ])
        s_new.append(jnp.sum(kn * qg, axis=-1, keepdims=True) + bias_ref[g, BQ])
    mx = s_new[0]
    for g in range(G):
        mx = jnp.maximum(mx, jnp.maximum(s_new[g], jnp.max(s_cache[g], axis=0)))
    den = jnp.zeros_like(mx)
    o = jnp.zeros(kn.shape, F32)
    for g in range(G):
        pc = jnp.exp(s_cache[g] - mx)
        pn = jnp.exp(s_new[g] - mx)
        den = den + jnp.sum(pc, axis=0) + pn
        o = o + jnp.sum(pc * cvs[g][0], axis=0) + pn * vn
    o_ref[0] = (o / den).astype(o_ref.dtype)


def _attn_sample(q, k_new, v_new, cache_k, cache_v, bias_rows, *, H, hd):
    Bd, Lc = cache_k.shape[:2]
    G = len(DSW_GROUPS)
    BQ = ATTN_BLOCK
    views_k, views_v, specs = [], [], []
    for w, d in DSW_GROUPS:
        assert w // d == BQ and Lc % (d * BQ) == 0
        views_k.append(cache_k.reshape(Bd, Lc // d, d, H, hd))
        views_v.append(cache_v.reshape(Bd, Lc // d, d, H, hd))
        last = Lc // d // BQ - 1
        specs.append(pl.BlockSpec((1, BQ, None, H, hd),
                                  functools.partial(lambda b, last: (b, last, 0, 0, 0), last=last)))
    one = lambda b: (b, 0, 0)
    return pl.pallas_call(
        functools.partial(_attn_sample_kernel, G=G),
        grid=(Bd,),
        in_specs=[pl.BlockSpec((1, G, H, hd), lambda b: (b, 0, 0, 0)),
                  pl.BlockSpec((1, H, hd), one), pl.BlockSpec((1, H, hd), one),
                  *specs, *specs,
                  pl.BlockSpec(bias_rows.shape, lambda b: (0, 0, 0, 0))],
        out_specs=pl.BlockSpec((1, H, hd), one),
        out_shape=jax.ShapeDtypeStruct((Bd, H, hd), BF16),
        compiler_params=_params("parallel"),
        name="attn_sample",
    )(q.reshape(Bd, G, H, hd), k_new.reshape(Bd, H, hd), v_new.reshape(Bd, H, hd),
      *views_k, *views_v, bias_rows)


def _router_kernel(h_ref, w_ref, o_ref, *, n_experts):
    logits = _dot_exact(h_ref[...], w_ref[...])
    lane = lax.broadcasted_iota(jnp.int32, logits.shape, 1)
    lanef = lane.astype(F32)
    logits = jnp.where(lane < n_experts, logits, -jnp.inf)
    v1 = jnp.max(logits, axis=-1, keepdims=True)
    i1 = jnp.min(jnp.where(logits == v1, lanef, float(LANES)), axis=-1, keepdims=True)
    rest = jnp.where(lanef == i1, -jnp.inf, logits)
    v2 = jnp.max(rest, axis=-1, keepdims=True)
    i2 = jnp.min(jnp.where(rest == v2, lanef, float(LANES)), axis=-1, keepdims=True)
    e2 = jnp.exp(v2 - v1)
    w1 = 1.0 / (1.0 + e2)
    w2 = e2 / (1.0 + e2)
    o_ref[...] = jnp.where(lane == 0, i1, jnp.where(lane == 1, i2, jnp.where(lane == 2, w1, w2)))


def _router(h, w_router_p, n_experts, tm=512):
    M, D = h.shape
    tm = min(tm, M)
    assert M % tm == 0
    out = pl.pallas_call(
        functools.partial(_router_kernel, n_experts=n_experts),
        grid=(M // tm,),
        in_specs=[pl.BlockSpec((tm, D), lambda i: (i, 0)), pl.BlockSpec((D, LANES), lambda i: (0, 0))],
        out_specs=pl.BlockSpec((tm, LANES), lambda i: (i, 0)),
        out_shape=jax.ShapeDtypeStruct((M, LANES), F32),
        compiler_params=_params("parallel"),
        name="router",
    )(h, w_router_p)
    return out[:, 0:2].astype(jnp.int32), out[:, 2:4]


def _row_copy(src_ref, dst_ref, sem, src_row, dst_row):
    return pltpu.make_async_copy(src_ref.at[pl.ds(src_row, 1), :], dst_ref.at[pl.ds(dst_row, 1), :], sem)


DMA_UNROLL = 8


def _gather_kernel(idx_ref, src_ref, o_ref, sem, *, R):
    base = pl.program_id(0) * R

    def start(r, c):
        _row_copy(src_ref, o_ref, sem, idx_ref[base + r], r).start()
        return c

    lax.fori_loop(0, R, start, 0, unroll=math.gcd(DMA_UNROLL, R))

    def wait(r, c):
        _row_copy(src_ref, o_ref, sem, 0, r).wait()
        return c

    lax.fori_loop(0, R, wait, 0, unroll=math.gcd(DMA_UNROLL, R))


def _gather_rows(src, idx, R=512):
    M = idx.shape[0]
    D = src.shape[1]
    R = min(R, M)
    assert M % R == 0
    return pl.pallas_call(
        functools.partial(_gather_kernel, R=R),
        grid_spec=pltpu.PrefetchScalarGridSpec(
            num_scalar_prefetch=1, grid=(M // R,),
            in_specs=[pl.BlockSpec(memory_space=pl.ANY)],
            out_specs=pl.BlockSpec((R, D), lambda i, idx: (i, 0)),
            scratch_shapes=[pltpu.SemaphoreType.DMA(())]),
        out_shape=jax.ShapeDtypeStruct((M, D), src.dtype),
        compiler_params=_params("arbitrary"),
        name="moe_dispatch",
    )(idx, src)


def _combine_ln_kernel(p0_ref, p1_ref, rows_ref, w_ref, h_ref, g_ref, b_ref, o_ref, buf0, buf1, sem, *, R, alpha):
    base = pl.program_id(0) * R

    def start(r, c):
        _row_copy(rows_ref, buf0, sem, p0_ref[base + r], r).start()
        _row_copy(rows_ref, buf1, sem, p1_ref[base + r], r).start()
        return c

    lax.fori_loop(0, R, start, 0, unroll=math.gcd(DMA_UNROLL, R))

    def wait(r, c):
        _row_copy(rows_ref, buf0, sem, 0, r).wait()
        _row_copy(rows_ref, buf1, sem, 0, r).wait()
        return c

    lax.fori_loop(0, R, wait, 0, unroll=math.gcd(DMA_UNROLL, R))
    w = w_ref[...]
    moe = w[:, 0:1] * buf0[...] + w[:, 1:2] * buf1[...]
    o_ref[...] = _layer_norm(alpha * h_ref[...] + moe, g_ref[...], b_ref[...])


def _combine_ln(rows, p0, p1, w, h, g, b, alpha, R=256):
    M, D = h.shape
    R = min(R, M)
    assert M % R == 0
    row = lambda i, p0, p1: (i, 0)
    fixed = lambda i, p0, p1: (0, 0)
    return pl.pallas_call(
        functools.partial(_combine_ln_kernel, R=R, alpha=alpha),
        grid_spec=pltpu.PrefetchScalarGridSpec(
            num_scalar_prefetch=2, grid=(M // R,),
            in_specs=[pl.BlockSpec(memory_space=pl.ANY), pl.BlockSpec((R, TOP_K), row), pl.BlockSpec((R, D), row),
                      pl.BlockSpec((1, D), fixed), pl.BlockSpec((1, D), fixed)],
            out_specs=pl.BlockSpec((R, D), row),
            scratch_shapes=[pltpu.VMEM((R, D), F32), pltpu.VMEM((R, D), F32), pltpu.SemaphoreType.DMA(())]),
        out_shape=jax.ShapeDtypeStruct((M, D), F32),
        compiler_params=_params("arbitrary"),
        name="moe_combine_ln",
    )(p0, p1, rows, w, h, g.reshape(1, D), b.reshape(1, D))


def _routing_tables(top_i, n_experts, tm):
    nt = top_i.shape[0]
    na = TOP_K * nt
    e = top_i.reshape(na)
    onehot = (e[None, :] == jnp.arange(n_experts, dtype=jnp.int32)[:, None]).astype(jnp.int32)
    csum = jnp.cumsum(onehot, axis=1)
    counts = csum[:, -1]
    padded = ((counts + tm - 1) // tm) * tm
    pend = jnp.cumsum(padded)
    pstart = pend - padded
    slot = jnp.sum(onehot * (csum - 1 + pstart[:, None]), axis=0)
    ntiles = (na + n_experts * (tm - 1) + tm - 1) // tm
    mpad = ntiles * tm
    src = jnp.zeros((mpad,), jnp.int32).at[slot].set(
        jnp.arange(na, dtype=jnp.int32) // TOP_K, unique_indices=True, indices_are_sorted=False)
    tile_start = jnp.arange(ntiles, dtype=jnp.int32) * tm
    used = (tile_start < pend[-1]).astype(jnp.int32)
    te = jnp.minimum(jnp.sum((tile_start[:, None] >= pend[None, :]).astype(jnp.int32), axis=1), n_experts - 1)
    last_used = jnp.max(jnp.where(used > 0, te, 0))
    te = jnp.where(used > 0, te, last_used)
    slot2 = slot.reshape(nt, TOP_K)
    return src, te, used, slot2[:, 0], slot2[:, 1]


def _t5_bucket_ids(dist, max_dist):
    max_exact = N_BUCKETS // 2
    d = np.asarray(dist, dtype=np.int64)
    log_ratio = np.log(np.maximum(d, max_exact) / max_exact) / math.log(max_dist / max_exact)
    large = np.minimum(max_exact + (log_ratio * (N_BUCKETS - max_exact)).astype(np.int64), N_BUCKETS - 1)
    return np.where(d < max_exact, d, large).astype(np.int32)


def _group_biases(rel_bias, H):
    max_dist = max(w for w, _ in DSW_GROUPS)
    out = []
    for g, (w, d) in enumerate(DSW_GROUPS):
        buckets = _t5_bucket_ids(d * np.arange(w // d + 1), max_dist)
        out.append(rel_bias[buckets][:, g * H:(g + 1) * H].astype(F32))
    return out


def _prompt_bias_vecs(biases):
    BQ = ATTN_BLOCK
    vecs = []
    for bg in biases:
        v = jnp.concatenate([bg[::-1], jnp.full((2 * BQ - 1, bg.shape[1]), NEG_INF, F32)], axis=0)
        vecs.append(jnp.transpose(v)[:, None, :])
    return jnp.stack(vecs)


def _sample_bias_rows(biases):
    BQ = ATTN_BLOCK
    rows = []
    for bg in biases:
        rows.append(jnp.concatenate([bg[::-1][:BQ], bg[0:1], jnp.zeros((7, bg.shape[1]), F32)], axis=0))
    return jnp.stack(rows)[..., None]


def kernel(x_prompt, x_sample, state_gla, cache_k, cache_v, ln_g, ln_b, gla_w_in, gla_w_a2, gla_b_a,
           gla_norm_g, gla_w_out, kv_w, dsw_w_q, dsw_w_out, rel_bias, ffn_w_gu, ffn_w_down,
           moe_w_router, moe_w_gu, moe_w_down):
    B, T, D = x_prompt.shape
    Bd = x_sample.shape[0]
    depth = ln_g.shape[0]
    assert depth == 2 and x_sample.shape[1] == 1
    alpha = (2 * depth) ** 0.25
    rank, dkt = gla_w_a2.shape[1:]
    dv = gla_norm_g.shape[1]
    dvt = gla_w_out.shape[1]
    Hg = dvt // dv
    dk = dkt // Hg
    G = len(DSW_GROUPS)
    Ha = rel_bias.shape[1] // G
    hd = D // Ha
    n_experts = moe_w_router.shape[2]
    f_dense = ffn_w_down.shape[1]
    f_exp = moe_w_down.shape[2]
    N = B * T

    w_in = gla_w_in[0]
    c_a = 2 * dkt + dvt
    w_in_all = jnp.concatenate([w_in[:, :c_a], w_in[:, c_a + rank:],
                                jnp.pad(w_in[:, c_a:c_a + rank], ((0, 0), (0, LANES - rank)))], axis=1).astype(BF16)
    n_in = w_in_all.shape[1]
    tn_in = max(t for t in range(LANES, 1024 + 1, LANES) if n_in % t == 0)
    w_a2p = jnp.pad(gla_w_a2[0], ((0, LANES - rank), (0, 0)))
    w_a2tp = jnp.pad(gla_w_a2[0].T, ((0, 0), (0, LANES - rank)))
    w_gout = gla_w_out[0].astype(BF16)
    w_k = kv_w[:, :D].astype(BF16)
    w_v = kv_w[:, D:].astype(BF16)
    w_q = dsw_w_q[0].astype(BF16)
    w_aout = dsw_w_out[0].astype(BF16)
    w_ffn_gu = ffn_w_gu.astype(BF16)
    w_ffn_down = ffn_w_down.astype(BF16)
    w_moe_gu = moe_w_gu[0].astype(BF16)
    w_moe_down = moe_w_down[0].astype(BF16)
    w_router_p = jnp.pad(moe_w_router[0], ((0, 0), (0, LANES - n_experts)))
    biases = _group_biases(rel_bias, Ha)
    tf_dense = math.gcd(f_dense, 512)
    tf_exp = math.gcd(f_exp, 1024)

    def dense_tables(m, tm):
        n = m // tm
        return jnp.zeros((n,), jnp.int32), jnp.ones((n,), jnp.int32)

    def layer0_in(x2):
        return _mm(x2, w_in_all, tn=tn_in, name="gla_in_proj")

    def layer0_out(o_gla, x2, by_head):
        m = x2.shape[0]
        h1, h1b = _mm_res_ln(o_gla, w_gout, x2, ln_g[0, 0], ln_b[0, 0], alpha, name="gla_out_ln")
        tm = min(512, m)
        te, us = dense_tables(m, tm)
        h2, h2b = _ffn(h1b, w_ffn_gu, w_ffn_down, te, us, tm=tm, tf=tf_dense, mode="ln", alpha=alpha,
                       res=h1, g=ln_g[0, 1], b=ln_b[0, 1], name="ffn_dense")
        k = _mm(h2b, w_k, by_head=by_head, name="k_proj")
        v = _mm(h2b, w_v, by_head=by_head, name="v_proj")
        q = _mm(h2b, w_q, scale=hd ** -0.5, natural=not by_head, by_head=by_head, name="q_proj")
        return h2, k, v, q

    xp = x_prompt.reshape(N, D)
    proj_p = layer0_in(xp)
    o_gla_p, state_p = _gla_prompt(proj_p, w_a2p, gla_b_a[0], gla_norm_g[0], B=B, T=T, H=Hg, dk=dk, dv=dv)
    h2_p, (k_p, kh_p), (v_p, vh_p), qh_p = layer0_out(o_gla_p, xp, True)
    o_att_p = _attn_prompt(qh_p, kh_p, vh_p, _prompt_bias_vecs(biases), B=B, T=T, H=Ha, hd=hd)
    h3_p, _ = _mm_res_ln(o_att_p.reshape(N, D), w_aout, h2_p, ln_g[1, 0], ln_b[1, 0], alpha, name="attn_out_ln")

    xs = x_sample.reshape(Bd, D)
    proj_s = layer0_in(xs)
    o_gla_s, state_s = _gla_sample(proj_s[:, :n_in - LANES], proj_s[:, n_in - LANES:], w_a2tp, gla_b_a[0],
                                   gla_norm_g[0], state_gla[0], H=Hg, dk=dk, dv=dv)
    h2_s, k_s, v_s, q_s = layer0_out(o_gla_s.reshape(Bd, dvt), xs, False)
    o_att_s = _attn_sample(q_s, k_s, v_s, cache_k, cache_v, _sample_bias_rows(biases), H=Ha, hd=hd)
    h3_s, _ = _mm_res_ln(o_att_s.reshape(Bd, D), w_aout, h2_s, ln_g[1, 0], ln_b[1, 0], alpha, name="attn_out_ln")

    ti_p, tw_p = _router(h3_p, w_router_p, n_experts)
    ti_s, tw_s = _router(h3_s, w_router_p, n_experts)
    tm_moe = 512
    src, te, used, p0, p1 = _routing_tables(jnp.concatenate([ti_p, ti_s]), n_experts, tm_moe)
    h3_all = jnp.concatenate([h3_p, h3_s], axis=0)
    x_sorted = _gather_rows(h3_all, src)
    y_sorted = _ffn(x_sorted, w_moe_gu, w_moe_down, te, used, tm=tm_moe, tf=tf_exp, mode="plain", name="ffn_moe")
    y_p = _combine_ln(y_sorted, p0[:N], p1[:N], tw_p, h3_p, ln_g[1, 1], ln_b[1, 1], alpha)
    y_s = _combine_ln(y_sorted, p0[N:], p1[N:], tw_s, h3_s, ln_g[1, 1], ln_b[1, 1], alpha)

    return (y_p.reshape(B, T, D), y_s.reshape(Bd, 1, D),
            state_p[None], state_s[None],
            k_p.reshape(B, T, Ha, hd), v_p.reshape(B, T, Ha, hd),
            k_s.reshape(Bd, 1, Ha, hd), v_s.reshape(Bd, 1, Ha, hd))
```

```python
import functools
import math

import jax
import jax.numpy as jnp
import numpy as np
from jax import lax
from jax.experimental import pallas as pl
from jax.experimental.pallas import tpu as pltpu

F32 = jnp.float32
BF16 = jnp.bfloat16

GLA_TAU = 16.0
GLA_CHUNK = 64
LN_EPS = 1e-5
NEG_INF = -1e30
DSW_GROUPS = ((128, 1), (512, 4), (2048, 16))
N_BUCKETS = 32
TOP_K = 2

LANES = 128
VMEM_LIMIT_BYTES = 56 * 1024 * 1024
ATTN_BLOCK = 128


def _params(*sem):
    return pltpu.CompilerParams(dimension_semantics=sem, vmem_limit_bytes=VMEM_LIMIT_BYTES)


def _layer_norm(x, g, b):
    mu = jnp.mean(x, axis=-1, keepdims=True)
    xc = x - mu
    var = jnp.mean(xc * xc, axis=-1, keepdims=True)
    return xc * lax.rsqrt(var + LN_EPS) * g + b


def _dot(a, b):
    return jnp.dot(a, b, preferred_element_type=F32)


def _dot_nt(a, b):
    return lax.dot_general(a, b, (((1,), (1,)), ((), ())), preferred_element_type=F32)


def _dot_tn(a, b):
    return lax.dot_general(a, b, (((0,), (0,)), ((), ())), preferred_element_type=F32)


def _dot_exact(a, b):
    return jnp.dot(a, b, preferred_element_type=F32, precision=lax.Precision.HIGHEST)


def _dot_hi_lo(a, b):
    a_hi = a.astype(BF16)
    b_hi = b.astype(BF16)
    a_lo = (a - a_hi.astype(F32)).astype(BF16)
    b_lo = (b - b_hi.astype(F32)).astype(BF16)
    return _dot(a_hi, b_hi) + (_dot(a_lo, b_hi) + _dot(a_hi, b_lo))


def _mm_kernel(a_ref, w_ref, o_ref, *, scale):
    acc = _dot(a_ref[...].astype(BF16), w_ref[...])
    if scale != 1.0:
        acc = acc * scale
    o_ref[...] = acc


def _mm(a, w, scale=1.0, tm=1024, tn=1024, name="mm"):
    M, K = a.shape
    N = w.shape[1]
    tm = min(tm, M)
    tn = math.gcd(tn, N)
    assert M % tm == 0 and N % tn == 0 and tn % LANES == 0, (M, N, tm, tn)
    return pl.pallas_call(
        functools.partial(_mm_kernel, scale=scale),
        grid=(M // tm, N // tn),
        in_specs=[pl.BlockSpec((tm, K), lambda i, j: (i, 0)),
                  pl.BlockSpec((K, tn), lambda i, j: (0, j))],
        out_specs=pl.BlockSpec((tm, tn), lambda i, j: (i, j)),
        out_shape=jax.ShapeDtypeStruct((M, N), F32),
        compiler_params=_params("parallel", "arbitrary"),
        name=name,
    )(a, w)


def _top2_route(logits, n_experts):
    lane = lax.broadcasted_iota(jnp.int32, logits.shape, 1)
    lanef = lane.astype(F32)
    logits = jnp.where(lane < n_experts, logits, -jnp.inf)
    v1 = jnp.max(logits, axis=-1, keepdims=True)
    i1 = jnp.min(jnp.where(logits == v1, lanef, float(LANES)), axis=-1, keepdims=True)
    rest = jnp.where(lanef == i1, -jnp.inf, logits)
    v2 = jnp.max(rest, axis=-1, keepdims=True)
    i2 = jnp.min(jnp.where(rest == v2, lanef, float(LANES)), axis=-1, keepdims=True)
    e2 = jnp.exp(v2 - v1)
    w1 = 1.0 / (1.0 + e2)
    w2 = e2 / (1.0 + e2)
    return jnp.where(lane == 0, i1, jnp.where(lane == 1, i2, jnp.where(lane == 2, w1, w2)))


def _mm_res_ln_kernel(*refs, alpha, want_bf16, n_experts, aliased):
    refs = list(refs)
    a_ref, w_ref, res_ref, g_ref, b_ref = refs[:5]
    del refs[:5]
    if n_experts:
        wr_ref = refs.pop(0)
    if aliased:
        refs.pop(0)
    o_ref = refs.pop(0)
    mix = _dot(a_ref[...], w_ref[...])
    y = _layer_norm(alpha * res_ref[...] + mix, g_ref[...], b_ref[...])
    o_ref[...] = y
    if want_bf16:
        refs.pop(0)[...] = y.astype(BF16)
    if n_experts:
        refs.pop(0)[...] = _top2_route(_dot_exact(y, wr_ref[...]), n_experts)


def _mm_res_ln(a, w, res, g, b, alpha, *, tm=512, want_bf16=True, router_w=None, n_experts=0,
               extra_rows=0, into=None, name="mm_res_ln"):
    M, K = a.shape
    N = w.shape[1]
    tm = min(tm, M)
    assert M % tm == 0 and extra_rows <= tm
    nsteps = M // tm + (1 if extra_rows else 0)
    row = lambda i: (jnp.minimum(i, M // tm - 1), 0)
    fixed = lambda i: (0, 0)
    in_specs = [pl.BlockSpec((tm, K), row), pl.BlockSpec((K, N), fixed),
                pl.BlockSpec((tm, N), row), pl.BlockSpec((1, N), fixed), pl.BlockSpec((1, N), fixed)]
    args = [a, w, res, g.reshape(1, N), b.reshape(1, N)]
    if n_experts:
        in_specs.append(pl.BlockSpec((N, LANES), fixed))
        args.append(router_w)
    aliases = {}
    rows, off = M + extra_rows, 0
    if into is not None:
        rows = into.shape[0]
        assert (rows - M) % tm == 0
        off = (rows - M) // tm
        aliases = {len(args): 0}
        in_specs.append(pl.BlockSpec(memory_space=pl.ANY))
        args.append(into)
    out_specs = [pl.BlockSpec((tm, N), lambda i: (i + off, 0))]
    out_shape = [jax.ShapeDtypeStruct((rows, N), F32)]
    if want_bf16:
        out_specs.append(pl.BlockSpec((tm, N), row))
        out_shape.append(jax.ShapeDtypeStruct((M, N), BF16))
    if n_experts:
        out_specs.append(pl.BlockSpec((tm, LANES), row))
        out_shape.append(jax.ShapeDtypeStruct((M, LANES), F32))
    outs = pl.pallas_call(
        functools.partial(_mm_res_ln_kernel, alpha=alpha, want_bf16=want_bf16, n_experts=n_experts,
                          aliased=into is not None),
        grid=(nsteps,),
        in_specs=in_specs, out_specs=out_specs, out_shape=out_shape,
        input_output_aliases=aliases,
        compiler_params=_params("arbitrary" if extra_rows else "parallel"),
        name=name,
    )(*args)
    return outs[0] if len(outs) == 1 else outs


def _ffn_kernel(te_ref, used_ref, x_ref, wg_ref, wu_ref, wd_ref, *rest, mode, alpha, nf):
    if mode == "ln":
        res_ref, g_ref, b_ref, o_ref, obf_ref = rest
    else:
        (o_ref,) = rest
    i = pl.program_id(0)
    f = pl.program_id(1)

    @pl.when(f == 0)
    def _():
        o_ref[...] = jnp.zeros_like(o_ref)

    @pl.when(used_ref[i] != 0)
    def _():
        x = x_ref[...].astype(BF16)
        g = _dot(x, wg_ref[0])
        u = _dot(x, wu_ref[0])
        act = (g * jax.nn.sigmoid(g) * u).astype(BF16)
        o_ref[...] += _dot(act, wd_ref[0])

    if mode == "ln":
        @pl.when(f == nf - 1)
        def _():
            y = _layer_norm(alpha * res_ref[...] + o_ref[...], g_ref[...], b_ref[...])
            o_ref[...] = y
            obf_ref[...] = y.astype(BF16)


def _ffn(x, w_gu, w_down, tile_expert, tile_used, *, tm, tf, mode, alpha=1.0,
         res=None, g=None, b=None, name="ffn"):
    M, D = x.shape
    E, F, _ = w_down.shape
    assert M % tm == 0 and F % tf == 0
    nf = F // tf
    row = lambda i, f, te, us: (i, 0)
    fixed = lambda i, f, te, us: (0, 0)
    in_specs = [
        pl.BlockSpec((tm, D), row),
        pl.BlockSpec((1, D, tf), lambda i, f, te, us: (te[i], 0, f * us[i])),
        pl.BlockSpec((1, D, tf), lambda i, f, te, us: (te[i], 0, nf + f * us[i])),
        pl.BlockSpec((1, tf, D), lambda i, f, te, us: (te[i], f * us[i], 0)),
    ]
    if mode == "ln":
        in_specs += [pl.BlockSpec((tm, D), row), pl.BlockSpec((1, D), fixed), pl.BlockSpec((1, D), fixed)]
        extra = (res, g.reshape(1, D), b.reshape(1, D))
        out_specs = [pl.BlockSpec((tm, D), row), pl.BlockSpec((tm, D), row)]
        out_shape = [jax.ShapeDtypeStruct((M, D), F32), jax.ShapeDtypeStruct((M, D), BF16)]
    else:
        extra = ()
        out_specs = pl.BlockSpec((tm, D), row)
        out_shape = jax.ShapeDtypeStruct((M, D), F32)
    return pl.pallas_call(
        functools.partial(_ffn_kernel, mode=mode, alpha=alpha, nf=nf),
        grid_spec=pltpu.PrefetchScalarGridSpec(
            num_scalar_prefetch=2, grid=(M // tm, nf), in_specs=in_specs, out_specs=out_specs),
        out_shape=out_shape,
        compiler_params=_params("parallel", "arbitrary"),
        name=name,
    )(tile_expert, tile_used, x, w_gu, w_gu, w_down, *extra)


def _moe_ffn_kernel(te_ref, used_ref, src_ref, h_ref, wg_ref, wu_ref, wd_ref, o_ref, xbuf, sem, *, nf, tm):
    i = pl.program_id(0)
    f = pl.program_id(1)
    share = -(-tm // nf)
    slot = i % 2

    def row_copy(src_slot, buf, row):
        return pltpu.make_async_copy(h_ref.at[pl.ds(src_ref[src_slot], 1), :],
                                     xbuf.at[buf, pl.ds(row, 1), :], sem)

    @pl.when(jnp.logical_and(i == 0, f == 0))
    def _():
        def start(r, c):
            row_copy(r, 0, r).start()
            return c

        def wait(r, c):
            row_copy(0, 0, r).wait()
            return c

        lax.fori_loop(0, tm, start, 0, unroll=math.gcd(DMA_UNROLL, tm))
        lax.fori_loop(0, tm, wait, 0, unroll=math.gcd(DMA_UNROLL, tm))

    @pl.when(f == 0)
    def _():
        o_ref[...] = jnp.zeros_like(o_ref)

    @pl.when(used_ref[i] != 0)
    def _():
        copies = [row_copy((i + 1) * (share * nf) + f * share + r, 1 - slot, f * share + r) for r in range(share)]
        for cp in copies:
            cp.start()
        x = xbuf[slot, 0:tm, :].astype(BF16)
        g = _dot(x, wg_ref[0])
        u = _dot(x, wu_ref[0])
        act = (g * jax.nn.sigmoid(g) * u).astype(BF16)
        o_ref[...] += _dot(act, wd_ref[0])
        for cp in copies:
            cp.wait()


def _moe_ffn(h, src, w_gu, w_down, tile_expert, tile_used, *, tm, tf, name="ffn_moe"):
    D = h.shape[1]
    E, F, _ = w_down.shape
    ntiles = tile_expert.shape[0]
    assert src.shape[0] == (ntiles + 1) * tm and F % tf == 0
    nf = F // tf
    stride = -(-tm // nf) * nf
    src = jnp.pad(src.reshape(ntiles + 1, tm), ((0, 0), (0, stride - tm))).reshape(-1)
    buf_rows = -(-stride // 8) * 8
    return pl.pallas_call(
        functools.partial(_moe_ffn_kernel, nf=nf, tm=tm),
        grid_spec=pltpu.PrefetchScalarGridSpec(
            num_scalar_prefetch=3, grid=(ntiles, nf),
            in_specs=[
                pl.BlockSpec(memory_space=pl.ANY),
                pl.BlockSpec((1, D, tf), lambda i, f, te, us, src: (te[i], 0, f * us[i])),
                pl.BlockSpec((1, D, tf), lambda i, f, te, us, src: (te[i], 0, nf + f * us[i])),
                pl.BlockSpec((1, tf, D), lambda i, f, te, us, src: (te[i], f * us[i], 0)),
            ],
            out_specs=pl.BlockSpec((tm, D), lambda i, f, te, us, src: (i, 0)),
            scratch_shapes=[pltpu.VMEM((2, buf_rows, D), F32), pltpu.SemaphoreType.DMA(())]),
        out_shape=jax.ShapeDtypeStruct((ntiles * tm, D), F32),
        compiler_params=_params("arbitrary", "arbitrary"),
        name=name,
    )(tile_expert, tile_used, src, h, w_gu, w_gu, w_down)


def _log_sigmoid(z):
    return jnp.minimum(z, 0.0) - jnp.log(1.0 + jnp.exp(-jnp.abs(z)))


def _gla_prompt_kernel(q_ref, k_ref, v_ref, r_ref, alr_ref, wa_ref, ba_ref, ng_ref, o_ref, s_ref,
                       *, nchunk, scale):
    C = GLA_CHUNK
    dk = q_ref.shape[1]

    @pl.when(pl.program_id(2) == 0)
    def _():
        s_ref[...] = jnp.zeros_like(s_ref)

    cb = nchunk * C
    dv = v_ref.shape[1]
    z = _dot_hi_lo(alr_ref[...], wa_ref[...]) + ba_ref[...]
    bcum = _log_sigmoid(z) / GLA_TAU
    row_in_chunk = lax.broadcasted_iota(jnp.int32, (cb, dk), 0) % C
    shift = 1
    while shift < C:
        bcum = bcum + jnp.where(row_in_chunk >= shift, pltpu.roll(bcum, shift, 0), 0.0)
        shift *= 2
    bc3 = bcum.reshape(nchunk, C, dk)
    b_last = bc3[:, C - 1:C, :]
    k3 = k_ref[...].reshape(nchunk, C, dk)
    v3 = v_ref[...].astype(BF16).reshape(nchunk, C, dv)
    q_dec = (q_ref[...].reshape(nchunk, C, dk) * scale * jnp.exp(bc3)).astype(BF16)
    k_dec = (k3 * jnp.exp(-bc3)).astype(BF16)
    k_end = (k3 * jnp.exp(b_last - bc3)).astype(BF16)
    causal = lax.broadcasted_iota(jnp.int32, (C, C), 0) >= lax.broadcasted_iota(jnp.int32, (C, C), 1)
    a = jnp.where(causal, jnp.einsum("nid,njd->nij", q_dec, k_dec, preferred_element_type=F32), 0.0)
    o = jnp.einsum("nij,nje->nie", a.astype(BF16), v3, preferred_element_type=F32)
    decay_rows = jnp.concatenate([jnp.exp(b_last).reshape(nchunk, dk), jnp.ones((LANES - nchunk, dk), F32)], axis=0)
    decay_cols = jnp.transpose(decay_rows)
    s = s_ref[0, 0]
    s_starts = []
    for j in range(nchunk):
        s_starts.append(s.astype(BF16))
        s = s * decay_cols[:, j:j + 1] + _dot_tn(k_end[j], v3[j])
    s_ref[0, 0] = s
    o = o + jnp.einsum("nid,nde->nie", q_dec, jnp.stack(s_starts), preferred_element_type=F32)
    o = o.reshape(cb, dv)
    on = o * lax.rsqrt(jnp.mean(o * o, axis=-1, keepdims=True) + LN_EPS) * ng_ref[...]
    rr = r_ref[...]
    o_ref[...] = (on * (rr * jax.nn.sigmoid(rr))).astype(o_ref.dtype)


def _gla_prompt(proj, w_a2p, b_a, norm_g, *, B, T, H, dk, dv, cb=512):
    N = B * T
    cb = min(cb, T)
    assert T % cb == 0 and cb % GLA_CHUNK == 0
    nc = T // cb
    dkt = H * dk
    dvt = H * dv
    assert (2 * dkt) % dv == 0 and (2 * dkt + 2 * dvt) % LANES == 0
    v0 = (2 * dkt) // dv
    r0 = (2 * dkt + dvt) // dv
    a0 = (2 * dkt + 2 * dvt) // LANES
    rowblk = lambda b, h, c: b * nc + c
    return pl.pallas_call(
        functools.partial(_gla_prompt_kernel, nchunk=cb // GLA_CHUNK, scale=dk ** -0.5),
        grid=(B, H, nc),
        in_specs=[
            pl.BlockSpec((cb, dk), lambda b, h, c: (rowblk(b, h, c), h)),
            pl.BlockSpec((cb, dk), lambda b, h, c: (rowblk(b, h, c), H + h)),
            pl.BlockSpec((cb, dv), lambda b, h, c: (rowblk(b, h, c), v0 + h)),
            pl.BlockSpec((cb, dv), lambda b, h, c: (rowblk(b, h, c), r0 + h)),
            pl.BlockSpec((cb, LANES), lambda b, h, c: (rowblk(b, h, c), a0)),
            pl.BlockSpec((LANES, dk), lambda b, h, c: (0, h)),
            pl.BlockSpec((1, dk), lambda b, h, c: (0, h)),
            pl.BlockSpec((1, dv), lambda b, h, c: (0, 0)),
        ],
        out_specs=[
            pl.BlockSpec((cb, dv), lambda b, h, c: (rowblk(b, h, c), h)),
            pl.BlockSpec((1, 1, dk, dv), lambda b, h, c: (b, h, 0, 0)),
        ],
        out_shape=[jax.ShapeDtypeStruct((N, dvt), BF16), jax.ShapeDtypeStruct((B, H, dk, dv), F32)],
        compiler_params=_params("parallel", "parallel", "arbitrary"),
        name="gla_prompt",
    )(proj, proj, proj, proj, proj, w_a2p, b_a.reshape(1, dkt), norm_g.reshape(1, dv))


def _gla_sample_kernel(q_ref, k_ref, v_ref, r_ref, alr_ref, wat_ref, ba_ref, ng_ref, s0_ref, o_ref, s_ref,
                       *, scale):
    z = jnp.sum(wat_ref[...] * alr_ref[0], axis=-1, keepdims=True) + ba_ref[...]
    decay = jnp.exp(_log_sigmoid(z) / GLA_TAU)
    s_new = s0_ref[0, 0] * decay + k_ref[0, 0] * v_ref[0]
    s_ref[0, 0] = s_new
    o = jnp.sum((q_ref[0, 0] * scale) * s_new, axis=0, keepdims=True)
    on = o * lax.rsqrt(jnp.mean(o * o, axis=-1, keepdims=True) + LN_EPS) * ng_ref[...]
    rr = r_ref[0]
    o_ref[0] = (on * (rr * jax.nn.sigmoid(rr))).astype(o_ref.dtype)


def _gla_sample(qkvr, alr, w_a2tp, b_a, norm_g, s0, *, H, dk, dv):
    Bd = qkvr.shape[0]
    dkt = H * dk
    dvt = H * dv
    q = qkvr[:, :dkt].reshape(Bd, H, dk, 1)
    k = qkvr[:, dkt:2 * dkt].reshape(Bd, H, dk, 1)
    v = qkvr[:, 2 * dkt:2 * dkt + dvt].reshape(Bd, 1, dvt)
    r = qkvr[:, 2 * dkt + dvt:].reshape(Bd, 1, dvt)
    col = lambda b, h: (b, h, 0, 0)
    return pl.pallas_call(
        functools.partial(_gla_sample_kernel, scale=dk ** -0.5),
        grid=(Bd, H),
        in_specs=[
            pl.BlockSpec((1, 1, dk, 1), col),
            pl.BlockSpec((1, 1, dk, 1), col),
            pl.BlockSpec((1, 1, dv), lambda b, h: (b, 0, h)),
            pl.BlockSpec((1, 1, dv), lambda b, h: (b, 0, h)),
            pl.BlockSpec((1, 1, LANES), lambda b, h: (b, 0, 0)),
            pl.BlockSpec((dk, LANES), lambda b, h: (h, 0)),
            pl.BlockSpec((dk, 1), lambda b, h: (h, 0)),
            pl.BlockSpec((1, dv), lambda b, h: (0, 0)),
            pl.BlockSpec((1, 1, dk, dv), col),
        ],
        out_specs=[
            pl.BlockSpec((1, 1, dv), lambda b, h: (b, 0, h)),
            pl.BlockSpec((1, 1, dk, dv), col),
        ],
        out_shape=[jax.ShapeDtypeStruct((Bd, 1, dvt), BF16), jax.ShapeDtypeStruct((Bd, H, dk, dv), F32)],
        compiler_params=_params("parallel", "parallel"),
        name="gla_sample",
    )(q, k, v, r, alr.reshape(Bd, 1, LANES), w_a2tp, b_a.reshape(dkt, 1), norm_g.reshape(1, dv), s0)


def _attn_prompt_kernel(q0_ref, q1_ref, q2_ref, k_ref, v_ref, bvec_ref, o_ref, qb, kb, vb, m_s, l_s, acc_s,
                        *, T, dils):
    BQ = ATTN_BLOCK
    q_refs = (q0_ref, q1_ref, q2_ref)
    hd = k_ref.shape[2]
    nblk = T // BQ

    def rows_of(start, n, d):
        return pl.ds(start, n) if d == 1 else pl.ds(start, n, stride=d)

    def softmax_pv(s, v):
        m = jnp.max(s, axis=-1, keepdims=True)
        p = jnp.exp(s - m)
        l = jnp.sum(p, axis=-1, keepdims=True)
        pv = jnp.einsum("bqk,bkd->bqd", p.astype(BF16), v, preferred_element_type=F32)
        return m, l, pv

    for g, d in enumerate(dils):
        nb = T // (d * BQ)
        bias = pltpu.roll(jnp.broadcast_to(bvec_ref[g, 0], (BQ, 3 * BQ)), 0, 1, stride=1, stride_axis=0)
        blocks = [(r, 0) for r in range(d)] + [(r, i) for r in range(d) for i in range(1, nb)]
        nfirst = d
        for e, (r, i) in enumerate(blocks):
            qb[e] = q_refs[g][0, rows_of(r + i * BQ * d, BQ, d), :].astype(BF16)
            if i == 0:
                kb[e, 0:BQ, :] = k_ref[0, rows_of(r, BQ, d), :].astype(BF16)
                vb[e, 0:BQ, :] = v_ref[0, rows_of(r, BQ, d), :].astype(BF16)
            else:
                kb[e] = k_ref[0, rows_of(r + (i - 1) * BQ * d, 2 * BQ, d), :].astype(BF16)
                vb[e] = v_ref[0, rows_of(r + (i - 1) * BQ * d, 2 * BQ, d), :].astype(BF16)
        s_first = jnp.einsum("bqd,bkd->bqk", qb[0:nfirst], kb[0:nfirst, 0:BQ, :],
                             preferred_element_type=F32) + bias[:, BQ:2 * BQ]
        stats = [softmax_pv(s_first, vb[0:nfirst, 0:BQ, :])]
        if nb > 1:
            s_rest = jnp.einsum("bqd,bkd->bqk", qb[nfirst:nblk], kb[nfirst:nblk],
                                preferred_element_type=F32) + bias[:, 0:2 * BQ]
            stats.append(softmax_pv(s_rest, vb[nfirst:nblk]))
        for e, (r, i) in enumerate(blocks):
            m, l, pv = stats[0] if e < nfirst else stats[1]
            idx = e if e < nfirst else e - nfirst
            rows = rows_of(r + i * BQ * d, BQ, d)
            m_s[g, rows, :] = jnp.broadcast_to(m[idx], (BQ, hd))
            l_s[g, rows, :] = jnp.broadcast_to(l[idx], (BQ, hd))
            acc_s[g, rows, :] = pv[idx]

    CH = 2 * BQ
    G = len(dils)
    for c in range(T // CH):
        sl = slice(c * CH, (c + 1) * CH)
        ms = [m_s[g, sl, :] for g in range(G)]
        mm = functools.reduce(jnp.maximum, ms)
        ws = [jnp.exp(m - mm) for m in ms]
        num = functools.reduce(lambda a, b: a + b, [ws[g] * acc_s[g, sl, :] for g in range(G)])
        den = functools.reduce(lambda a, b: a + b, [ws[g] * l_s[g, sl, :] for g in range(G)])
        o_ref[0, sl, :] = (num / den).astype(o_ref.dtype)


def _attn_prompt(q, k, v, bias_vecs, *, B, T, H, hd):
    dils = tuple(d for _, d in DSW_GROUPS)
    G = len(dils)
    BQ = ATTN_BLOCK
    for w, d in DSW_GROUPS:
        assert w // d == BQ and T % (d * BQ) == 0
    assert hd == LANES
    nblk = T // BQ
    qspec = lambda g: pl.BlockSpec((1, T, hd), lambda b, h: (b, 0, g * H + h))
    kvspec = pl.BlockSpec((1, T, hd), lambda b, h: (b, 0, h))
    return pl.pallas_call(
        functools.partial(_attn_prompt_kernel, T=T, dils=dils),
        grid=(B, H),
        in_specs=[qspec(0), qspec(1), qspec(2), kvspec, kvspec,
                  pl.BlockSpec((G, 1, 1, 3 * BQ), lambda b, h: (0, h, 0, 0))],
        out_specs=pl.BlockSpec((1, T, hd), lambda b, h: (b, 0, h)),
        out_shape=jax.ShapeDtypeStruct((B, T, H * hd), BF16),
        scratch_shapes=[pltpu.VMEM((nblk, BQ, hd), BF16), pltpu.VMEM((nblk, 2 * BQ, hd), BF16),
                        pltpu.VMEM((nblk, 2 * BQ, hd), BF16),
                        pltpu.VMEM((G, T, hd), F32), pltpu.VMEM((G, T, hd), F32), pltpu.VMEM((G, T, hd), F32)],
        compiler_params=_params("parallel", "parallel"),
        name="attn_prompt",
    )(q, q, q, k, v, bias_vecs)


def _attn_sample_kernel(q_ref, kn_ref, vn_ref, ck0, ck1, ck2, cv0, cv1, cv2, bias_ref, o_ref, *, G):
    cks = (ck0, ck1, ck2)
    cvs = (cv0, cv1, cv2)
    BQ = ATTN_BLOCK
    kn = kn_ref[0]
    vn = vn_ref[0]
    s_cache, s_new = [], []
    for g in range(G):
        qg = q_ref[0, g]
        s_cache.append(jnp.sum(cks[g][0] * qg[None], axis=-1, keepdims=True) + bias_ref[g, 0:BQ])
        s_new.append(jnp.sum(kn * qg, axis=-1, keepdims=True) + bias_ref[g, BQ])
    mx = s_new[0]
    for g in range(G):
        mx = jnp.maximum(mx, jnp.maximum(s_new[g], jnp.max(s_cache[g], axis=0)))
    den = jnp.zeros_like(mx)
    o = jnp.zeros(kn.shape, F32)
    for g in range(G):
        pc = jnp.exp(s_cache[g] - mx)
        pn = jnp.exp(s_new[g] - mx)
        den = den + jnp.sum(pc, axis=0) + pn
        o = o + jnp.sum(pc * cvs[g][0], axis=0) + pn * vn
    o_ref[0] = (o / den).astype(o_ref.dtype)


def _attn_sample(q, k_new, v_new, cache_k, cache_v, bias_rows, *, H, hd):
    Bd, Lc = cache_k.shape[:2]
    G = len(DSW_GROUPS)
    BQ = ATTN_BLOCK
    views_k, views_v, specs = [], [], []
    for w, d in DSW_GROUPS:
        assert w // d == BQ and Lc % (d * BQ) == 0
        views_k.append(cache_k.reshape(Bd, Lc // d, d, H, hd))
        views_v.append(cache_v.reshape(Bd, Lc // d, d, H, hd))
        last = Lc // d // BQ - 1
        specs.append(pl.BlockSpec((1, BQ, None, H, hd),
                                  functools.partial(lambda b, last: (b, last, 0, 0, 0), last=last)))
    one = lambda b: (b, 0, 0)
    return pl.pallas_call(
        functools.partial(_attn_sample_kernel, G=G),
        grid=(Bd,),
        in_specs=[pl.BlockSpec((1, G, H, hd), lambda b: (b, 0, 0, 0)),
                  pl.BlockSpec((1, H, hd), one), pl.BlockSpec((1, H, hd), one),
                  *specs, *specs,
                  pl.BlockSpec(bias_rows.shape, lambda b: (0, 0, 0, 0))],
        out_specs=pl.BlockSpec((1, H, hd), one),
        out_shape=jax.ShapeDtypeStruct((Bd, H, hd), BF16),
        compiler_params=_params("parallel"),
        name="attn_sample",
    )(q.reshape(Bd, G, H, hd), k_new.reshape(Bd, H, hd), v_new.reshape(Bd, H, hd),
      *views_k, *views_v, bias_rows)


def _row_copy(src_ref, dst_ref, sem, src_row, dst_row):
    return pltpu.make_async_copy(src_ref.at[pl.ds(src_row, 1), :], dst_ref.at[pl.ds(dst_row, 1), :], sem)


DMA_UNROLL = 8


def _combine_ln_kernel(p0_ref, p1_ref, rows_ref, w_ref, h_ref, g_ref, b_ref, o_ref, buf0, buf1, sem, *, R, alpha):
    base = pl.program_id(0) * R

    def start(r, c):
        _row_copy(rows_ref, buf0, sem, p0_ref[base + r], r).start()
        _row_copy(rows_ref, buf1, sem, p1_ref[base + r], r).start()
        return c

    lax.fori_loop(0, R, start, 0, unroll=math.gcd(DMA_UNROLL, R))

    def wait(r, c):
        _row_copy(rows_ref, buf0, sem, 0, r).wait()
        _row_copy(rows_ref, buf1, sem, 0, r).wait()
        return c

    lax.fori_loop(0, R, wait, 0, unroll=math.gcd(DMA_UNROLL, R))
    w = w_ref[...]
    moe = w[:, 0:1] * buf0[...] + w[:, 1:2] * buf1[...]
    o_ref[...] = _layer_norm(alpha * h_ref[...] + moe, g_ref[...], b_ref[...])


def _combine_ln(rows, p0, p1, w, h, g, b, alpha, R=256):
    M = p0.shape[0]
    D = h.shape[1]
    R = min(R, M)
    assert M % R == 0 and h.shape[0] >= M
    row = lambda i, p0, p1: (i, 0)
    fixed = lambda i, p0, p1: (0, 0)
    return pl.pallas_call(
        functools.partial(_combine_ln_kernel, R=R, alpha=alpha),
        grid_spec=pltpu.PrefetchScalarGridSpec(
            num_scalar_prefetch=2, grid=(M // R,),
            in_specs=[pl.BlockSpec(memory_space=pl.ANY), pl.BlockSpec((R, TOP_K), row), pl.BlockSpec((R, D), row),
                      pl.BlockSpec((1, D), fixed), pl.BlockSpec((1, D), fixed)],
            out_specs=pl.BlockSpec((R, D), row),
            scratch_shapes=[pltpu.VMEM((R, D), F32), pltpu.VMEM((R, D), F32), pltpu.SemaphoreType.DMA(())]),
        out_shape=jax.ShapeDtypeStruct((M, D), F32),
        compiler_params=_params("arbitrary"),
        name="moe_combine_ln",
    )(p0, p1, rows, w, h, g.reshape(1, D), b.reshape(1, D))


def _routing_tables(top_i, n_experts, tm):
    nt = top_i.shape[0]
    na = TOP_K * nt
    e = top_i.reshape(na)
    onehot = (e[None, :] == jnp.arange(n_experts, dtype=jnp.int32)[:, None]).astype(jnp.int32)
    csum = jnp.cumsum(onehot, axis=1)
    counts = csum[:, -1]
    padded = ((counts + tm - 1) // tm) * tm
    pend = jnp.cumsum(padded)
    pstart = pend - padded
    slot = jnp.sum(onehot * (csum - 1 + pstart[:, None]), axis=0)
    ntiles = (na + n_experts * (tm - 1) + tm - 1) // tm
    src = jnp.zeros(((ntiles + 1) * tm,), jnp.int32).at[slot].set(
        jnp.arange(na, dtype=jnp.int32) // TOP_K, unique_indices=True, indices_are_sorted=False)
    tile_start = jnp.arange(ntiles, dtype=jnp.int32) * tm
    used = (tile_start < pend[-1]).astype(jnp.int32)
    te = jnp.minimum(jnp.sum((tile_start[:, None] >= pend[None, :]).astype(jnp.int32), axis=1), n_experts - 1)
    last_used = jnp.max(jnp.where(used > 0, te, 0))
    te = jnp.where(used > 0, te, last_used)
    slot2 = slot.reshape(nt, TOP_K)
    return src, te, used, slot2[:, 0], slot2[:, 1]


def _t5_bucket_ids(dist, max_dist):
    max_exact = N_BUCKETS // 2
    d = np.asarray(dist, dtype=np.int64)
    log_ratio = np.log(np.maximum(d, max_exact) / max_exact) / math.log(max_dist / max_exact)
    large = np.minimum(max_exact + (log_ratio * (N_BUCKETS - max_exact)).astype(np.int64), N_BUCKETS - 1)
    return np.where(d < max_exact, d, large).astype(np.int32)


def _group_biases(rel_bias, H):
    max_dist = max(w for w, _ in DSW_GROUPS)
    out = []
    for g, (w, d) in enumerate(DSW_GROUPS):
        buckets = _t5_bucket_ids(d * np.arange(w // d + 1), max_dist)
        out.append(rel_bias[buckets][:, g * H:(g + 1) * H].astype(F32))
    return out


def _prompt_bias_vecs(biases):
    BQ = ATTN_BLOCK
    vecs = []
    for bg in biases:
        v = jnp.concatenate([bg[::-1], jnp.full((2 * BQ - 1, bg.shape[1]), NEG_INF, F32)], axis=0)
        vecs.append(jnp.transpose(v)[:, None, :])
    return jnp.stack(vecs)


def _sample_bias_rows(biases):
    BQ = ATTN_BLOCK
    rows = []
    for bg in biases:
        rows.append(jnp.concatenate([bg[::-1][:BQ], bg[0:1], jnp.zeros((7, bg.shape[1]), F32)], axis=0))
    return jnp.stack(rows)[..., None]


def kernel(x_prompt, x_sample, state_gla, cache_k, cache_v, ln_g, ln_b, gla_w_in, gla_w_a2, gla_b_a,
           gla_norm_g, gla_w_out, kv_w, dsw_w_q, dsw_w_out, rel_bias, ffn_w_gu, ffn_w_down,
           moe_w_router, moe_w_gu, moe_w_down):
    B, T, D = x_prompt.shape
    Bd = x_sample.shape[0]
    depth = ln_g.shape[0]
    assert depth == 2 and x_sample.shape[1] == 1
    alpha = (2 * depth) ** 0.25
    rank, dkt = gla_w_a2.shape[1:]
    dv = gla_norm_g.shape[1]
    dvt = gla_w_out.shape[1]
    Hg = dvt // dv
    dk = dkt // Hg
    G = len(DSW_GROUPS)
    Ha = rel_bias.shape[1] // G
    hd = D // Ha
    n_experts = moe_w_router.shape[2]
    f_dense = ffn_w_down.shape[1]
    f_exp = moe_w_down.shape[2]
    N = B * T

    w_in = gla_w_in[0]
    c_a = 2 * dkt + dvt
    w_in_all = jnp.concatenate([w_in[:, :c_a], w_in[:, c_a + rank:],
                                jnp.pad(w_in[:, c_a:c_a + rank], ((0, 0), (0, LANES - rank)))], axis=1).astype(BF16)
    n_in = w_in_all.shape[1]
    tn_in = max(t for t in range(LANES, 1024 + 1, LANES) if n_in % t == 0)
    w_a2p = jnp.pad(gla_w_a2[0], ((0, LANES - rank), (0, 0)))
    w_a2tp = jnp.pad(gla_w_a2[0].T, ((0, 0), (0, LANES - rank)))
    w_gout = gla_w_out[0].astype(BF16)
    w_k = kv_w[:, :D].astype(BF16)
    w_v = kv_w[:, D:].astype(BF16)
    w_q = dsw_w_q[0].astype(BF16)
    w_aout = dsw_w_out[0].astype(BF16)
    w_ffn_gu = ffn_w_gu.astype(BF16)
    w_ffn_down = ffn_w_down.astype(BF16)
    w_moe_gu = moe_w_gu[0].astype(BF16)
    w_moe_down = moe_w_down[0].astype(BF16)
    w_router_p = jnp.pad(moe_w_router[0], ((0, 0), (0, LANES - n_experts)))
    biases = _group_biases(rel_bias, Ha)
    tf_dense = math.gcd(f_dense, 512)
    tm_moe = 512
    tf_exp = max(t for t in range(256, 1024 + 1, 256) if f_exp % t == 0)

    def dense_tables(m, tm):
        n = m // tm
        return jnp.zeros((n,), jnp.int32), jnp.ones((n,), jnp.int32)

    def layer0_in(x2):
        return _mm(x2, w_in_all, tn=tn_in, name="gla_in_proj")

    def layer0_out(o_gla, x2):
        m = x2.shape[0]
        h1, h1b = _mm_res_ln(o_gla, w_gout, x2, ln_g[0, 0], ln_b[0, 0], alpha, name="gla_out_ln")
        tm = min(512, m)
        te, us = dense_tables(m, tm)
        h2, h2b = _ffn(h1b, w_ffn_gu, w_ffn_down, te, us, tm=tm, tf=tf_dense, mode="ln", alpha=alpha,
                       res=h1, g=ln_g[0, 1], b=ln_b[0, 1], name="ffn_dense")
        k = _mm(h2b, w_k, name="k_proj")
        v = _mm(h2b, w_v, name="v_proj")
        q = _mm(h2b, w_q, scale=hd ** -0.5, name="q_proj")
        return h2, k, v, q

    def attn_out(o_att, h2, **kw):
        return _mm_res_ln(o_att, w_aout, h2, ln_g[1, 0], ln_b[1, 0], alpha, want_bf16=False,
                          router_w=w_router_p, n_experts=n_experts, name="attn_out_ln", **kw)

    xp = x_prompt.reshape(N, D)
    proj_p = layer0_in(xp)
    o_gla_p, state_p = _gla_prompt(proj_p, w_a2p, gla_b_a[0], gla_norm_g[0], B=B, T=T, H=Hg, dk=dk, dv=dv)
    h2_p, k_p, v_p, q_p = layer0_out(o_gla_p, xp)
    o_att_p = _attn_prompt(q_p.reshape(B, T, G * D), k_p.reshape(B, T, D), v_p.reshape(B, T, D),
                           _prompt_bias_vecs(biases), B=B, T=T, H=Ha, hd=hd)
    h3, route_p = attn_out(o_att_p.reshape(N, D), h2_p, extra_rows=Bd)

    xs = x_sample.reshape(Bd, D)
    proj_s = layer0_in(xs)
    o_gla_s, state_s = _gla_sample(proj_s[:, :n_in - LANES], proj_s[:, n_in - LANES:], w_a2tp, gla_b_a[0],
                                   gla_norm_g[0], state_gla[0], H=Hg, dk=dk, dv=dv)
    h2_s, k_s, v_s, q_s = layer0_out(o_gla_s.reshape(Bd, dvt), xs)
    o_att_s = _attn_sample(q_s, k_s, v_s, cache_k, cache_v, _sample_bias_rows(biases), H=Ha, hd=hd)
    h3, route_s = attn_out(o_att_s.reshape(Bd, D), h2_s, into=h3)

    route = jnp.concatenate([route_p, route_s], axis=0)
    top_i = route[:, 0:TOP_K].astype(jnp.int32)
    top_w = route[:, TOP_K:2 * TOP_K]
    src, te, used, p0, p1 = _routing_tables(top_i, n_experts, tm_moe)
    y_sorted = _moe_ffn(h3, src, w_moe_gu, w_moe_down, te, used, tm=tm_moe, tf=tf_exp)
    y_p = _combine_ln(y_sorted, p0[:N], p1[:N], top_w[:N], h3, ln_g[1, 1], ln_b[1, 1], alpha)
    y_s = _combine_ln(y_sorted, p0[N:], p1[N:], top_w[N:], h3[N:], ln_g[1, 1], ln_b[1, 1], alpha)

    return (y_p.reshape(B, T, D), y_s.reshape(Bd, 1, D),
            state_p[None], state_s[None],
            k_p.reshape(B, T, Ha, hd), v_p.reshape(B, T, Ha, hd),
            k_s.reshape(Bd, 1, Ha, hd), v_s.reshape(Bd, 1, Ha, hd))
```

```python
import functools
import math

import jax
import jax.numpy as jnp
import numpy as np
from jax import lax
from jax.experimental import pallas as pl
from jax.experimental.pallas import tpu as pltpu

F32 = jnp.float32
BF16 = jnp.bfloat16

GLA_TAU = 16.0
GLA_CHUNK = 64
LN_EPS = 1e-5
NEG_INF = -1e30
DSW_GROUPS = ((128, 1), (512, 4), (2048, 16))
N_BUCKETS = 32
TOP_K = 2

LANES = 128
VMEM_LIMIT_BYTES = 56 * 1024 * 1024
ATTN_BLOCK = 128


def _params(*sem):
    return pltpu.CompilerParams(dimension_semantics=sem, vmem_limit_bytes=VMEM_LIMIT_BYTES)


def _layer_norm(x, g, b):
    mu = jnp.mean(x, axis=-1, keepdims=True)
    xc = x - mu
    var = jnp.mean(xc * xc, axis=-1, keepdims=True)
    return xc * lax.rsqrt(var + LN_EPS) * g + b


def _dot(a, b):
    return jnp.dot(a, b, preferred_element_type=F32)


def _dot_nt(a, b):
    return lax.dot_general(a, b, (((1,), (1,)), ((), ())), preferred_element_type=F32)


def _dot_tn(a, b):
    return lax.dot_general(a, b, (((0,), (0,)), ((), ())), preferred_element_type=F32)


def _dot_hi_lo(a, b):
    a_hi = a.astype(BF16)
    b_hi = b.astype(BF16)
    a_lo = (a - a_hi.astype(F32)).astype(BF16)
    b_lo = (b - b_hi.astype(F32)).astype(BF16)
    return _dot(a_hi, b_hi) + (_dot(a_lo, b_hi) + _dot(a_hi, b_lo))


def _mm_kernel(a_ref, w_ref, o_ref, *, scale):
    acc = _dot(a_ref[...].astype(BF16), w_ref[...])
    if scale != 1.0:
        acc = acc * scale
    o_ref[...] = acc


def _mm(a, w, scale=1.0, tm=1024, tn=1024, name="mm"):
    M, K = a.shape
    N = w.shape[1]
    tm = min(tm, M)
    tn = math.gcd(tn, N)
    assert M % tm == 0 and N % tn == 0 and tn % LANES == 0, (M, N, tm, tn)
    return pl.pallas_call(
        functools.partial(_mm_kernel, scale=scale),
        grid=(M // tm, N // tn),
        in_specs=[pl.BlockSpec((tm, K), lambda i, j: (i, 0)),
                  pl.BlockSpec((K, tn), lambda i, j: (0, j))],
        out_specs=pl.BlockSpec((tm, tn), lambda i, j: (i, j)),
        out_shape=jax.ShapeDtypeStruct((M, N), F32),
        compiler_params=_params("parallel", "arbitrary"),
        name=name,
    )(a, w)


def _top2_route(logits, n_experts):
    lane = lax.broadcasted_iota(jnp.int32, logits.shape, 1)
    lanef = lane.astype(F32)
    logits = jnp.where(lane < n_experts, logits, -jnp.inf)
    v1 = jnp.max(logits, axis=-1, keepdims=True)
    i1 = jnp.min(jnp.where(logits == v1, lanef, float(LANES)), axis=-1, keepdims=True)
    rest = jnp.where(lanef == i1, -jnp.inf, logits)
    v2 = jnp.max(rest, axis=-1, keepdims=True)
    i2 = jnp.min(jnp.where(rest == v2, lanef, float(LANES)), axis=-1, keepdims=True)
    e2 = jnp.exp(v2 - v1)
    w1 = 1.0 / (1.0 + e2)
    w2 = e2 / (1.0 + e2)
    return jnp.where(lane == 0, i1, jnp.where(lane == 1, i2, jnp.where(lane == 2, w1, w2)))


def _mm_res_ln_kernel(*refs, alpha, want_bf16, n_experts, aliased):
    refs = list(refs)
    a_ref, w_ref, res_ref, g_ref, b_ref = refs[:5]
    del refs[:5]
    if n_experts:
        wr_ref = refs.pop(0)
    if aliased:
        refs.pop(0)
    o_ref = refs.pop(0)
    mix = _dot(a_ref[...], w_ref[...])
    y = _layer_norm(alpha * res_ref[...] + mix, g_ref[...], b_ref[...])
    o_ref[...] = y
    if want_bf16:
        refs.pop(0)[...] = y.astype(BF16)
    if n_experts:
        y_hi = y.astype(BF16)
        y_lo = (y - y_hi.astype(F32)).astype(BF16)
        part = _dot(y_hi, wr_ref[...])
        logits = part[:, 0:LANES] + part[:, LANES:2 * LANES] + _dot(y_lo, wr_ref[:, 0:LANES])
        refs.pop(0)[...] = _top2_route(logits, n_experts)


def _mm_res_ln(a, w, res, g, b, alpha, *, tm=512, want_bf16=True, router_w=None, n_experts=0,
               extra_rows=0, into=None, name="mm_res_ln"):
    M, K = a.shape
    N = w.shape[1]
    tm = min(tm, M)
    assert M % tm == 0 and extra_rows <= tm
    nsteps = M // tm + (1 if extra_rows else 0)
    row = lambda i: (jnp.minimum(i, M // tm - 1), 0)
    fixed = lambda i: (0, 0)
    in_specs = [pl.BlockSpec((tm, K), row), pl.BlockSpec((K, N), fixed),
                pl.BlockSpec((tm, N), row), pl.BlockSpec((1, N), fixed), pl.BlockSpec((1, N), fixed)]
    args = [a, w, res, g.reshape(1, N), b.reshape(1, N)]
    if n_experts:
        in_specs.append(pl.BlockSpec((N, 2 * LANES), fixed))
        args.append(router_w)
    aliases = {}
    rows, off = M + extra_rows, 0
    if into is not None:
        rows = into.shape[0]
        assert (rows - M) % tm == 0
        off = (rows - M) // tm
        aliases = {len(args): 0}
        in_specs.append(pl.BlockSpec(memory_space=pl.ANY))
        args.append(into)
    out_specs = [pl.BlockSpec((tm, N), lambda i: (i + off, 0))]
    out_shape = [jax.ShapeDtypeStruct((rows, N), F32)]
    if want_bf16:
        out_specs.append(pl.BlockSpec((tm, N), row))
        out_shape.append(jax.ShapeDtypeStruct((M, N), BF16))
    if n_experts:
        out_specs.append(pl.BlockSpec((tm, LANES), row))
        out_shape.append(jax.ShapeDtypeStruct((M, LANES), F32))
    outs = pl.pallas_call(
        functools.partial(_mm_res_ln_kernel, alpha=alpha, want_bf16=want_bf16, n_experts=n_experts,
                          aliased=into is not None),
        grid=(nsteps,),
        in_specs=in_specs, out_specs=out_specs, out_shape=out_shape,
        input_output_aliases=aliases,
        compiler_params=_params("arbitrary" if extra_rows else "parallel"),
        name=name,
    )(*args)
    return outs[0] if len(outs) == 1 else outs


def _ffn_kernel(x_ref, wg_ref, wu_ref, wd_ref, res_ref, g_ref, b_ref, *rest, alpha, nf, has_side):
    if has_side:
        side_ref, o_ref, obf_ref, side_out_ref = rest
        side_out_ref[...] = side_ref[...].astype(BF16)
    else:
        o_ref, obf_ref = rest
    f = pl.program_id(1)

    @pl.when(f == 0)
    def _():
        o_ref[...] = jnp.zeros_like(o_ref)

    x = x_ref[...]
    g = _dot(x, wg_ref[...])
    u = _dot(x, wu_ref[...])
    act = (g * jax.nn.sigmoid(g) * u).astype(BF16)
    o_ref[...] += _dot(act, wd_ref[...])

    @pl.when(f == nf - 1)
    def _():
        y = _layer_norm(alpha * res_ref[...] + o_ref[...], g_ref[...], b_ref[...])
        o_ref[...] = y
        obf_ref[...] = y.astype(BF16)


def _ffn(x, w_gu, w_down, res, g, b, alpha, *, tm, tf, side=None, name="ffn"):
    M, D = x.shape
    F = w_down.shape[0]
    assert M % tm == 0 and F % tf == 0
    nf = F // tf
    nsteps = (M // tm) * nf
    row = lambda i, f: (i, 0)
    fixed = lambda i, f: (0, 0)
    in_specs = [
        pl.BlockSpec((tm, D), row),
        pl.BlockSpec((D, tf), lambda i, f: (0, f)),
        pl.BlockSpec((D, tf), lambda i, f: (0, nf + f)),
        pl.BlockSpec((tf, D), lambda i, f: (f, 0)),
        pl.BlockSpec((tm, D), row), pl.BlockSpec((1, D), fixed), pl.BlockSpec((1, D), fixed),
    ]
    args = [x, w_gu, w_gu, w_down, res, g.reshape(1, D), b.reshape(1, D)]
    out_specs = [pl.BlockSpec((tm, D), row), pl.BlockSpec((tm, D), row)]
    out_shape = [jax.ShapeDtypeStruct((M, D), F32), jax.ShapeDtypeStruct((M, D), BF16)]
    if side is not None:
        R, C = side.shape
        cr = min(c for c in range(16, R + 1, 16) if R % c == 0 and R // c <= nsteps)
        nchunks = R // cr
        chunk = lambda i, f: (jnp.minimum(i * nf + f, nchunks - 1), 0)
        in_specs.append(pl.BlockSpec((cr, C), chunk))
        args.append(side)
        out_specs.append(pl.BlockSpec((cr, C), chunk))
        out_shape.append(jax.ShapeDtypeStruct((R, C), BF16))
    return pl.pallas_call(
        functools.partial(_ffn_kernel, alpha=alpha, nf=nf, has_side=side is not None),
        grid=(M // tm, nf), in_specs=in_specs, out_specs=out_specs, out_shape=out_shape,
        compiler_params=_params("arbitrary" if side is not None else "parallel", "arbitrary"),
        name=name,
    )(*args)


def _moe_ffn_kernel(te_ref, used_ref, src_ref, h_ref, wg_ref, wu_ref, wd_ref, o_ref, xbuf, sem, *, nf, tm):
    i = pl.program_id(0)
    f = pl.program_id(1)
    share = -(-tm // nf)
    slot = i % 2

    def row_copy(src_slot, buf, row):
        return pltpu.make_async_copy(h_ref.at[pl.ds(src_ref[src_slot], 1), :],
                                     xbuf.at[buf, pl.ds(row, 1), :], sem)

    @pl.when(jnp.logical_and(i == 0, f == 0))
    def _():
        def start(r, c):
            row_copy(r, 0, r).start()
            return c

        def wait(r, c):
            row_copy(0, 0, r).wait()
            return c

        lax.fori_loop(0, tm, start, 0, unroll=math.gcd(DMA_UNROLL, tm))
        lax.fori_loop(0, tm, wait, 0, unroll=math.gcd(DMA_UNROLL, tm))

    @pl.when(f == 0)
    def _():
        o_ref[...] = jnp.zeros_like(o_ref)

    @pl.when(used_ref[i] != 0)
    def _():
        copies = [row_copy((i + 1) * (share * nf) + f * share + r, 1 - slot, f * share + r) for r in range(share)]
        for cp in copies:
            cp.start()
        x = xbuf[slot, 0:tm, :].astype(BF16)
        g = _dot(x, wg_ref[0])
        u = _dot(x, wu_ref[0])
        act = (g * jax.nn.sigmoid(g) * u).astype(BF16)
        o_ref[...] += _dot(act, wd_ref[0])
        for cp in copies:
            cp.wait()


def _moe_ffn(h, src, w_gu, w_down, tile_expert, tile_used, *, tm, tf, name="ffn_moe"):
    D = h.shape[1]
    E, F, _ = w_down.shape
    ntiles = tile_expert.shape[0]
    assert src.shape[0] == (ntiles + 1) * tm and F % tf == 0
    nf = F // tf
    stride = -(-tm // nf) * nf
    src = jnp.pad(src.reshape(ntiles + 1, tm), ((0, 0), (0, stride - tm))).reshape(-1)
    buf_rows = -(-stride // 8) * 8
    return pl.pallas_call(
        functools.partial(_moe_ffn_kernel, nf=nf, tm=tm),
        grid_spec=pltpu.PrefetchScalarGridSpec(
            num_scalar_prefetch=3, grid=(ntiles, nf),
            in_specs=[
                pl.BlockSpec(memory_space=pl.ANY),
                pl.BlockSpec((1, D, tf), lambda i, f, te, us, src: (te[i], 0, f * us[i])),
                pl.BlockSpec((1, D, tf), lambda i, f, te, us, src: (te[i], 0, nf + f * us[i])),
                pl.BlockSpec((1, tf, D), lambda i, f, te, us, src: (te[i], f * us[i], 0)),
            ],
            out_specs=pl.BlockSpec((tm, D), lambda i, f, te, us, src: (i, 0)),
            scratch_shapes=[pltpu.VMEM((2, buf_rows, D), F32), pltpu.SemaphoreType.DMA(())]),
        out_shape=jax.ShapeDtypeStruct((ntiles * tm, D), F32),
        compiler_params=_params("arbitrary", "arbitrary"),
        name=name,
    )(tile_expert, tile_used, src, h, w_gu, w_gu, w_down)


def _log_sigmoid(z):
    return jnp.minimum(z, 0.0) - jnp.log(1.0 + jnp.exp(-jnp.abs(z)))


def _gla_prompt_kernel(q_ref, k_ref, v_ref, r_ref, alr_ref, wa_ref, ba_ref, ng_ref, o_ref, s_ref,
                       *, nchunk, scale):
    C = GLA_CHUNK
    dk = q_ref.shape[1]

    @pl.when(pl.program_id(2) == 0)
    def _():
        s_ref[...] = jnp.zeros_like(s_ref)

    cb = nchunk * C
    dv = v_ref.shape[1]
    z = _dot_hi_lo(alr_ref[...], wa_ref[...]) + ba_ref[...]
    bcum = _log_sigmoid(z) / GLA_TAU
    row_in_chunk = lax.broadcasted_iota(jnp.int32, (cb, dk), 0) % C
    shift = 1
    while shift < C:
        bcum = bcum + jnp.where(row_in_chunk >= shift, pltpu.roll(bcum, shift, 0), 0.0)
        shift *= 2
    bc3 = bcum.reshape(nchunk, C, dk)
    b_last = bc3[:, C - 1:C, :]
    k3 = k_ref[...].reshape(nchunk, C, dk)
    v3 = v_ref[...].astype(BF16).reshape(nchunk, C, dv)
    q_dec = (q_ref[...].reshape(nchunk, C, dk) * scale * jnp.exp(bc3)).astype(BF16)
    k_dec = (k3 * jnp.exp(-bc3)).astype(BF16)
    k_end = (k3 * jnp.exp(b_last - bc3)).astype(BF16)
    causal = lax.broadcasted_iota(jnp.int32, (C, C), 0) >= lax.broadcasted_iota(jnp.int32, (C, C), 1)
    a = jnp.where(causal, jnp.einsum("nid,njd->nij", q_dec, k_dec, preferred_element_type=F32), 0.0)
    o = jnp.einsum("nij,nje->nie", a.astype(BF16), v3, preferred_element_type=F32)
    decay_rows = jnp.concatenate([jnp.exp(b_last).reshape(nchunk, dk), jnp.ones((LANES - nchunk, dk), F32)], axis=0)
    decay_cols = jnp.transpose(decay_rows)
    s = s_ref[0, 0]
    s_starts = []
    for j in range(nchunk):
        s_starts.append(s.astype(BF16))
        s = s * decay_cols[:, j:j + 1] + _dot_tn(k_end[j], v3[j])
    s_ref[0, 0] = s
    o = o + jnp.einsum("nid,nde->nie", q_dec, jnp.stack(s_starts), preferred_element_type=F32)
    o = o.reshape(cb, dv)
    on = o * lax.rsqrt(jnp.mean(o * o, axis=-1, keepdims=True) + LN_EPS) * ng_ref[...]
    rr = r_ref[...]
    o_ref[...] = (on * (rr * jax.nn.sigmoid(rr))).astype(o_ref.dtype)


def _gla_prompt(proj, w_a2p, b_a, norm_g, *, B, T, H, dk, dv, cb=512):
    N = B * T
    cb = min(cb, T)
    assert T % cb == 0 and cb % GLA_CHUNK == 0
    nc = T // cb
    dkt = H * dk
    dvt = H * dv
    assert (2 * dkt) % dv == 0 and (2 * dkt + 2 * dvt) % LANES == 0
    v0 = (2 * dkt) // dv
    r0 = (2 * dkt + dvt) // dv
    a0 = (2 * dkt + 2 * dvt) // LANES
    rowblk = lambda b, h, c: b * nc + c
    return pl.pallas_call(
        functools.partial(_gla_prompt_kernel, nchunk=cb // GLA_CHUNK, scale=dk ** -0.5),
        grid=(B, H, nc),
        in_specs=[
            pl.BlockSpec((cb, dk), lambda b, h, c: (rowblk(b, h, c), h)),
            pl.BlockSpec((cb, dk), lambda b, h, c: (rowblk(b, h, c), H + h)),
            pl.BlockSpec((cb, dv), lambda b, h, c: (rowblk(b, h, c), v0 + h)),
            pl.BlockSpec((cb, dv), lambda b, h, c: (rowblk(b, h, c), r0 + h)),
            pl.BlockSpec((cb, LANES), lambda b, h, c: (rowblk(b, h, c), a0)),
            pl.BlockSpec((LANES, dk), lambda b, h, c: (0, h)),
            pl.BlockSpec((1, dk), lambda b, h, c: (0, h)),
            pl.BlockSpec((1, dv), lambda b, h, c: (0, 0)),
        ],
        out_specs=[
            pl.BlockSpec((cb, dv), lambda b, h, c: (rowblk(b, h, c), h)),
            pl.BlockSpec((1, 1, dk, dv), lambda b, h, c: (b, h, 0, 0)),
        ],
        out_shape=[jax.ShapeDtypeStruct((N, dvt), BF16), jax.ShapeDtypeStruct((B, H, dk, dv), F32)],
        compiler_params=_params("parallel", "parallel", "arbitrary"),
        name="gla_prompt",
    )(proj, proj, proj, proj, proj, w_a2p, b_a.reshape(1, dkt), norm_g.reshape(1, dv))


def _gla_sample_kernel(q_ref, k_ref, v_ref, r_ref, alr_ref, wat_ref, ba_ref, ng_ref, s0_ref, o_ref, s_ref,
                       *, scale, H, dk, dv):
    for h in range(H):
        rows = slice(h * dk, (h + 1) * dk)
        cols = slice(h * dv, (h + 1) * dv)
        z = jnp.sum(wat_ref[rows, :] * alr_ref[0], axis=-1, keepdims=True) + ba_ref[rows, :]
        decay = jnp.exp(_log_sigmoid(z) / GLA_TAU)
        s_new = s0_ref[0, h] * decay + k_ref[0, h] * v_ref[0, :, cols]
        s_ref[0, h] = s_new
        o = jnp.sum((q_ref[0, h] * scale) * s_new, axis=0, keepdims=True)
        on = o * lax.rsqrt(jnp.mean(o * o, axis=-1, keepdims=True) + LN_EPS) * ng_ref[...]
        rr = r_ref[0, :, cols]
        o_ref[0, :, cols] = (on * (rr * jax.nn.sigmoid(rr))).astype(o_ref.dtype)


def _gla_sample(qkvr, alr, w_a2tp, b_a, norm_g, s0, *, H, dk, dv):
    Bd = qkvr.shape[0]
    dkt = H * dk
    dvt = H * dv
    q = qkvr[:, :dkt].reshape(Bd, H, dk, 1)
    k = qkvr[:, dkt:2 * dkt].reshape(Bd, H, dk, 1)
    v = qkvr[:, 2 * dkt:2 * dkt + dvt].reshape(Bd, 1, dvt)
    r = qkvr[:, 2 * dkt + dvt:].reshape(Bd, 1, dvt)
    col = lambda b: (b, 0, 0, 0)
    vec = lambda b: (b, 0, 0)
    fixed = lambda b: (0, 0)
    return pl.pallas_call(
        functools.partial(_gla_sample_kernel, scale=dk ** -0.5, H=H, dk=dk, dv=dv),
        grid=(Bd,),
        in_specs=[
            pl.BlockSpec((1, H, dk, 1), col),
            pl.BlockSpec((1, H, dk, 1), col),
            pl.BlockSpec((1, 1, dvt), vec),
            pl.BlockSpec((1, 1, dvt), vec),
            pl.BlockSpec((1, 1, LANES), vec),
            pl.BlockSpec((dkt, LANES), fixed),
            pl.BlockSpec((dkt, 1), fixed),
            pl.BlockSpec((1, dv), fixed),
            pl.BlockSpec((1, H, dk, dv), col),
        ],
        out_specs=[
            pl.BlockSpec((1, 1, dvt), vec),
            pl.BlockSpec((1, H, dk, dv), col),
        ],
        out_shape=[jax.ShapeDtypeStruct((Bd, 1, dvt), BF16), jax.ShapeDtypeStruct((Bd, H, dk, dv), F32)],
        compiler_params=_params("parallel"),
        name="gla_sample",
    )(q, k, v, r, alr.reshape(Bd, 1, LANES), w_a2tp, b_a.reshape(dkt, 1), norm_g.reshape(1, dv), s0)


def _attn_prompt_kernel(q0_ref, q1_ref, q2_ref, k_ref, v_ref, bvec_ref, o_ref, qb, kb, vb, m_s, l_s, acc_s,
                        *, T, dils):
    BQ = ATTN_BLOCK
    q_refs = (q0_ref, q1_ref, q2_ref)
    hd = k_ref.shape[2]
    nblk = T // BQ

    def rows_of(start, n, d):
        return pl.ds(start, n) if d == 1 else pl.ds(start, n, stride=d)

    def softmax_pv(s, v):
        m = jnp.max(s, axis=-1, keepdims=True)
        p = jnp.exp(s - m)
        l = jnp.sum(p, axis=-1, keepdims=True)
        pv = jnp.einsum("bqk,bkd->bqd", p.astype(BF16), v, preferred_element_type=F32)
        return m, l, pv

    for g, d in enumerate(dils):
        nb = T // (d * BQ)
        bias = pltpu.roll(jnp.broadcast_to(bvec_ref[g, 0], (BQ, 3 * BQ)), 0, 1, stride=1, stride_axis=0)
        blocks = [(r, 0) for r in range(d)] + [(r, i) for r in range(d) for i in range(1, nb)]
        nfirst = d
        for e, (r, i) in enumerate(blocks):
            qb[e] = q_refs[g][0, rows_of(r + i * BQ * d, BQ, d), :].astype(BF16)
            if i == 0:
                kb[e, 0:BQ, :] = k_ref[0, rows_of(r, BQ, d), :].astype(BF16)
                vb[e, 0:BQ, :] = v_ref[0, rows_of(r, BQ, d), :].astype(BF16)
            else:
                kb[e] = k_ref[0, rows_of(r + (i - 1) * BQ * d, 2 * BQ, d), :].astype(BF16)
                vb[e] = v_ref[0, rows_of(r + (i - 1) * BQ * d, 2 * BQ, d), :].astype(BF16)
        s_first = jnp.einsum("bqd,bkd->bqk", qb[0:nfirst], kb[0:nfirst, 0:BQ, :],
                             preferred_element_type=F32) + bias[:, BQ:2 * BQ]
        stats = [softmax_pv(s_first, vb[0:nfirst, 0:BQ, :])]
        if nb > 1:
            s_rest = jnp.einsum("bqd,bkd->bqk", qb[nfirst:nblk], kb[nfirst:nblk],
                                preferred_element_type=F32) + bias[:, 0:2 * BQ]
            stats.append(softmax_pv(s_rest, vb[nfirst:nblk]))
        for e, (r, i) in enumerate(blocks):
            m, l, pv = stats[0] if e < nfirst else stats[1]
            idx = e if e < nfirst else e - nfirst
            rows = rows_of(r + i * BQ * d, BQ, d)
            m_s[g, rows, :] = jnp.broadcast_to(m[idx], (BQ, hd))
            l_s[g, rows, :] = jnp.broadcast_to(l[idx], (BQ, hd))
            acc_s[g, rows, :] = pv[idx]

    CH = 2 * BQ
    G = len(dils)
    for c in range(T // CH):
        sl = slice(c * CH, (c + 1) * CH)
        ms = [m_s[g, sl, :] for g in range(G)]
        mm = functools.reduce(jnp.maximum, ms)
        ws = [jnp.exp(m - mm) for m in ms]
        num = functools.reduce(lambda a, b: a + b, [ws[g] * acc_s[g, sl, :] for g in range(G)])
        den = functools.reduce(lambda a, b: a + b, [ws[g] * l_s[g, sl, :] for g in range(G)])
        o_ref[0, sl, :] = (num / den).astype(o_ref.dtype)


def _attn_prompt(q, k, v, bias_vecs, *, B, T, H, hd):
    dils = tuple(d for _, d in DSW_GROUPS)
    G = len(dils)
    BQ = ATTN_BLOCK
    for w, d in DSW_GROUPS:
        assert w // d == BQ and T % (d * BQ) == 0
    assert hd == LANES
    nblk = T // BQ
    qspec = lambda g: pl.BlockSpec((1, T, hd), lambda b, h: (b, 0, g * H + h))
    kvspec = pl.BlockSpec((1, T, hd), lambda b, h: (b, 0, h))
    return pl.pallas_call(
        functools.partial(_attn_prompt_kernel, T=T, dils=dils),
        grid=(B, H),
        in_specs=[qspec(0), qspec(1), qspec(2), kvspec, kvspec,
                  pl.BlockSpec((G, 1, 1, 3 * BQ), lambda b, h: (0, h, 0, 0))],
        out_specs=pl.BlockSpec((1, T, hd), lambda b, h: (b, 0, h)),
        out_shape=jax.ShapeDtypeStruct((B, T, H * hd), BF16),
        scratch_shapes=[pltpu.VMEM((nblk, BQ, hd), BF16), pltpu.VMEM((nblk, 2 * BQ, hd), BF16),
                        pltpu.VMEM((nblk, 2 * BQ, hd), BF16),
                        pltpu.VMEM((G, T, hd), F32), pltpu.VMEM((G, T, hd), F32), pltpu.VMEM((G, T, hd), F32)],
        compiler_params=_params("parallel", "parallel"),
        name="attn_prompt",
    )(q, q, q, k, v, bias_vecs)


def _attn_sample_kernel(q_ref, kn_ref, vn_ref, ck0, ck1, ck2, cv0, cv1, cv2, bias_ref, o_ref, *, G):
    cks = (ck0, ck1, ck2)
    cvs = (cv0, cv1, cv2)
    BQ = ATTN_BLOCK
    kn = kn_ref[0]
    vn = vn_ref[0]
    s_cache, s_new = [], []
    for g in range(G):
        qg = q_ref[0, g]
        s_cache.append(jnp.sum(cks[g][0] * qg[None], axis=-1, keepdims=True) + bias_ref[g, 0:BQ])
        s_new.append(jnp.sum(kn * qg, axis=-1, keepdims=True) + bias_ref[g, BQ])
    mx = s_new[0]
    for g in range(G):
        mx = jnp.maximum(mx, jnp.maximum(s_new[g], jnp.max(s_cache[g], axis=0)))
    den = jnp.zeros_like(mx)
    o = jnp.zeros(kn.shape, F32)
    for g in range(G):
        pc = jnp.exp(s_cache[g] - mx)
        pn = jnp.exp(s_new[g] - mx)
        den = den + jnp.sum(pc, axis=0) + pn
        o = o + jnp.sum(pc * cvs[g][0], axis=0) + pn * vn
    o_ref[0] = (o / den).astype(o_ref.dtype)


def _attn_sample(q, k_new, v_new, cache_k, cache_v, bias_rows, *, H, hd):
    Bd, Lc = cache_k.shape[:2]
    G = len(DSW_GROUPS)
    BQ = ATTN_BLOCK
    views_k, views_v, specs = [], [], []
    for w, d in DSW_GROUPS:
        assert w // d == BQ and Lc % (d * BQ) == 0
        views_k.append(cache_k.reshape(Bd, Lc // d, d, H, hd))
        views_v.append(cache_v.reshape(Bd, Lc // d, d, H, hd))
        last = Lc // d // BQ - 1
        specs.append(pl.BlockSpec((1, BQ, None, H, hd),
                                  functools.partial(lambda b, last: (b, last, 0, 0, 0), last=last)))
    one = lambda b: (b, 0, 0)
    return pl.pallas_call(
        functools.partial(_attn_sample_kernel, G=G),
        grid=(Bd,),
        in_specs=[pl.BlockSpec((1, G, H, hd), lambda b: (b, 0, 0, 0)),
                  pl.BlockSpec((1, H, hd), one), pl.BlockSpec((1, H, hd), one),
                  *specs, *specs,
                  pl.BlockSpec(bias_rows.shape, lambda b: (0, 0, 0, 0))],
        out_specs=pl.BlockSpec((1, H, hd), one),
        out_shape=jax.ShapeDtypeStruct((Bd, H, hd), BF16),
        compiler_params=_params("parallel"),
        name="attn_sample",
    )(q.reshape(Bd, G, H, hd), k_new.reshape(Bd, H, hd), v_new.reshape(Bd, H, hd),
      *views_k, *views_v, bias_rows)


def _row_copy(src_ref, dst_ref, sem, src_row, dst_row):
    return pltpu.make_async_copy(src_ref.at[pl.ds(src_row, 1), :], dst_ref.at[pl.ds(dst_row, 1), :], sem)


DMA_UNROLL = 8


def _combine_ln_kernel(p0_ref, p1_ref, rows_ref, w_ref, h_ref, g_ref, b_ref, o_ref, buf0, buf1, sem, *, R, alpha):
    base = pl.program_id(0) * R

    def start(r, c):
        _row_copy(rows_ref, buf0, sem, p0_ref[base + r], r).start()
        _row_copy(rows_ref, buf1, sem, p1_ref[base + r], r).start()
        return c

    lax.fori_loop(0, R, start, 0, unroll=math.gcd(DMA_UNROLL, R))

    def wait(r, c):
        _row_copy(rows_ref, buf0, sem, 0, r).wait()
        _row_copy(rows_ref, buf1, sem, 0, r).wait()
        return c

    lax.fori_loop(0, R, wait, 0, unroll=math.gcd(DMA_UNROLL, R))
    w = w_ref[...]
    moe = w[:, 0:1] * buf0[...] + w[:, 1:2] * buf1[...]
    o_ref[...] = _layer_norm(alpha * h_ref[...] + moe, g_ref[...], b_ref[...])


def _combine_ln(rows, p0, p1, w, h, g, b, alpha, R=256):
    M = p0.shape[0]
    D = h.shape[1]
    R = min(R, M)
    assert M % R == 0 and h.shape[0] >= M
    row = lambda i, p0, p1: (i, 0)
    fixed = lambda i, p0, p1: (0, 0)
    return pl.pallas_call(
        functools.partial(_combine_ln_kernel, R=R, alpha=alpha),
        grid_spec=pltpu.PrefetchScalarGridSpec(
            num_scalar_prefetch=2, grid=(M // R,),
            in_specs=[pl.BlockSpec(memory_space=pl.ANY), pl.BlockSpec((R, TOP_K), row), pl.BlockSpec((R, D), row),
                      pl.BlockSpec((1, D), fixed), pl.BlockSpec((1, D), fixed)],
            out_specs=pl.BlockSpec((R, D), row),
            scratch_shapes=[pltpu.VMEM((R, D), F32), pltpu.VMEM((R, D), F32), pltpu.SemaphoreType.DMA(())]),
        out_shape=jax.ShapeDtypeStruct((M, D), F32),
        compiler_params=_params("arbitrary"),
        name="moe_combine_ln",
    )(p0, p1, rows, w, h, g.reshape(1, D), b.reshape(1, D))


def _routing_tables(top_i, n_experts, tm):
    nt = top_i.shape[0]
    na = TOP_K * nt
    e = top_i.reshape(na)
    onehot = (e[None, :] == jnp.arange(n_experts, dtype=jnp.int32)[:, None]).astype(jnp.int32)
    csum = jnp.cumsum(onehot, axis=1)
    counts = csum[:, -1]
    padded = ((counts + tm - 1) // tm) * tm
    pend = jnp.cumsum(padded)
    pstart = pend - padded
    slot = jnp.sum(onehot * (csum - 1 + pstart[:, None]), axis=0)
    ntiles = (na + n_experts * (tm - 1) + tm - 1) // tm
    src = jnp.zeros(((ntiles + 1) * tm,), jnp.int32).at[slot].set(
        jnp.arange(na, dtype=jnp.int32) // TOP_K, unique_indices=True, indices_are_sorted=False)
    tile_start = jnp.arange(ntiles, dtype=jnp.int32) * tm
    used = (tile_start < pend[-1]).astype(jnp.int32)
    te = jnp.minimum(jnp.sum((tile_start[:, None] >= pend[None, :]).astype(jnp.int32), axis=1), n_experts - 1)
    last_used = jnp.max(jnp.where(used > 0, te, 0))
    te = jnp.where(used > 0, te, last_used)
    slot2 = slot.reshape(nt, TOP_K)
    return src, te, used, slot2[:, 0], slot2[:, 1]


def _t5_bucket_ids(dist, max_dist):
    max_exact = N_BUCKETS // 2
    d = np.asarray(dist, dtype=np.int64)
    log_ratio = np.log(np.maximum(d, max_exact) / max_exact) / math.log(max_dist / max_exact)
    large = np.minimum(max_exact + (log_ratio * (N_BUCKETS - max_exact)).astype(np.int64), N_BUCKETS - 1)
    return np.where(d < max_exact, d, large).astype(np.int32)


def _group_biases(rel_bias, H):
    max_dist = max(w for w, _ in DSW_GROUPS)
    out = []
    for g, (w, d) in enumerate(DSW_GROUPS):
        buckets = _t5_bucket_ids(d * np.arange(w // d + 1), max_dist)
        out.append(rel_bias[buckets][:, g * H:(g + 1) * H].astype(F32))
    return out


def _prompt_bias_vecs(biases):
    BQ = ATTN_BLOCK
    vecs = []
    for bg in biases:
        v = jnp.concatenate([bg[::-1], jnp.full((2 * BQ - 1, bg.shape[1]), NEG_INF, F32)], axis=0)
        vecs.append(jnp.transpose(v)[:, None, :])
    return jnp.stack(vecs)


def _sample_bias_rows(biases):
    BQ = ATTN_BLOCK
    rows = []
    for bg in biases:
        rows.append(jnp.concatenate([bg[::-1][:BQ], bg[0:1], jnp.zeros((7, bg.shape[1]), F32)], axis=0))
    return jnp.stack(rows)[..., None]


def kernel(x_prompt, x_sample, state_gla, cache_k, cache_v, ln_g, ln_b, gla_w_in, gla_w_a2, gla_b_a,
           gla_norm_g, gla_w_out, kv_w, dsw_w_q, dsw_w_out, rel_bias, ffn_w_gu, ffn_w_down,
           moe_w_router, moe_w_gu, moe_w_down):
    B, T, D = x_prompt.shape
    Bd = x_sample.shape[0]
    depth = ln_g.shape[0]
    assert depth == 2 and x_sample.shape[1] == 1
    alpha = (2 * depth) ** 0.25
    rank, dkt = gla_w_a2.shape[1:]
    dv = gla_norm_g.shape[1]
    dvt = gla_w_out.shape[1]
    Hg = dvt // dv
    dk = dkt // Hg
    G = len(DSW_GROUPS)
    Ha = rel_bias.shape[1] // G
    hd = D // Ha
    n_experts = moe_w_router.shape[2]
    f_dense = ffn_w_down.shape[1]
    f_exp = moe_w_down.shape[2]
    N = B * T

    w_in = gla_w_in[0]
    c_a = 2 * dkt + dvt
    w_in_all = jnp.concatenate([w_in[:, :c_a], w_in[:, c_a + rank:],
                                jnp.pad(w_in[:, c_a:c_a + rank], ((0, 0), (0, LANES - rank)))], axis=1).astype(BF16)
    n_in = w_in_all.shape[1]
    tn_in = max(t for t in range(LANES, 1024 + 1, LANES) if n_in % t == 0)
    w_a2p = jnp.pad(gla_w_a2[0], ((0, LANES - rank), (0, 0)))
    w_a2tp = jnp.pad(gla_w_a2[0].T, ((0, 0), (0, LANES - rank)))
    w_gout = gla_w_out[0].astype(BF16)
    w_k = kv_w[:, :D].astype(BF16)
    w_v = kv_w[:, D:].astype(BF16)
    w_q = dsw_w_q[0].astype(BF16)
    w_aout = dsw_w_out[0].astype(BF16)
    w_ffn_gu = ffn_w_gu[0].astype(BF16)
    w_ffn_down = ffn_w_down[0].astype(BF16)
    w_moe_gu = moe_w_gu[0].astype(BF16)
    w_router = jnp.pad(moe_w_router[0], ((0, 0), (0, LANES - n_experts)))
    w_router_hi = w_router.astype(BF16)
    w_router_p = jnp.concatenate([w_router_hi, (w_router - w_router_hi.astype(F32)).astype(BF16)], axis=1)
    biases = _group_biases(rel_bias, Ha)
    tf_dense = math.gcd(f_dense, 512)
    tm_moe = 512
    tf_exp = max(t for t in range(256, 1024 + 1, 256) if f_exp % t == 0)

    def layer0_in(x2):
        return _mm(x2, w_in_all, tn=tn_in, name="gla_in_proj")

    def layer0_out(o_gla, x2, side=None):
        m = x2.shape[0]
        h1, h1b = _mm_res_ln(o_gla, w_gout, x2, ln_g[0, 0], ln_b[0, 0], alpha, name="gla_out_ln")
        h2, h2b, *side_out = _ffn(h1b, w_ffn_gu, w_ffn_down, h1, ln_g[0, 1], ln_b[0, 1], alpha,
                                  tm=min(512, m), tf=tf_dense, side=side, name="ffn_dense")
        k = _mm(h2b, w_k, name="k_proj")
        v = _mm(h2b, w_v, name="v_proj")
        q = _mm(h2b, w_q, scale=hd ** -0.5, name="q_proj")
        return (h2, k, v, q, *side_out)

    def attn_out(o_att, h2, **kw):
        return _mm_res_ln(o_att, w_aout, h2, ln_g[1, 0], ln_b[1, 0], alpha, want_bf16=False,
                          router_w=w_router_p, n_experts=n_experts, name="attn_out_ln", **kw)

    xp = x_prompt.reshape(N, D)
    proj_p = layer0_in(xp)
    o_gla_p, state_p = _gla_prompt(proj_p, w_a2p, gla_b_a[0], gla_norm_g[0], B=B, T=T, H=Hg, dk=dk, dv=dv)
    h2_p, k_p, v_p, q_p, w_moe_down = layer0_out(o_gla_p, xp, side=moe_w_down[0].reshape(n_experts * f_exp, D))
    w_moe_down = w_moe_down.reshape(n_experts, f_exp, D)
    o_att_p = _attn_prompt(q_p.reshape(B, T, G * D), k_p.reshape(B, T, D), v_p.reshape(B, T, D),
                           _prompt_bias_vecs(biases), B=B, T=T, H=Ha, hd=hd)
    h3, route_p = attn_out(o_att_p.reshape(N, D), h2_p, extra_rows=Bd)

    xs = x_sample.reshape(Bd, D)
    proj_s = layer0_in(xs)
    o_gla_s, state_s = _gla_sample(proj_s[:, :n_in - LANES], proj_s[:, n_in - LANES:], w_a2tp, gla_b_a[0],
                                   gla_norm_g[0], state_gla[0], H=Hg, dk=dk, dv=dv)
    h2_s, k_s, v_s, q_s = layer0_out(o_gla_s.reshape(Bd, dvt), xs)
    o_att_s = _attn_sample(q_s, k_s, v_s, cache_k, cache_v, _sample_bias_rows(biases), H=Ha, hd=hd)
    h3, route_s = attn_out(o_att_s.reshape(Bd, D), h2_s, into=h3)

    route = jnp.concatenate([route_p, route_s], axis=0)
    top_i = route[:, 0:TOP_K].astype(jnp.int32)
    top_w = route[:, TOP_K:2 * TOP_K]
    src, te, used, p0, p1 = _routing_tables(top_i, n_experts, tm_moe)
    y_sorted = _moe_ffn(h3, src, w_moe_gu, w_moe_down, te, used, tm=tm_moe, tf=tf_exp)
    y_p = _combine_ln(y_sorted, p0[:N], p1[:N], top_w[:N], h3, ln_g[1, 1], ln_b[1, 1], alpha)
    y_s = _combine_ln(y_sorted, p0[N:], p1[N:], top_w[N:], h3[N:], ln_g[1, 1], ln_b[1, 1], alpha)

    return (y_p.reshape(B, T, D), y_s.reshape(Bd, 1, D),
            state_p[None], state_s[None],
            k_p.reshape(B, T, Ha, hd), v_p.reshape(B, T, Ha, hd),
            k_s.reshape(Bd, 1, Ha, hd), v_s.reshape(Bd, 1, Ha, hd))
```

```python
import functools
import math

import jax
import jax.numpy as jnp
import numpy as np
from jax import lax
from jax.experimental import pallas as pl
from jax.experimental.pallas import tpu as pltpu

F32 = jnp.float32
BF16 = jnp.bfloat16

GLA_TAU = 16.0
GLA_CHUNK = 64
LN_EPS = 1e-5
NEG_INF = -1e30
DSW_GROUPS = ((128, 1), (512, 4), (2048, 16))
N_BUCKETS = 32
TOP_K = 2

LANES = 128
VMEM_LIMIT_BYTES = 56 * 1024 * 1024
ATTN_BLOCK = 128


def _params(*sem):
    return pltpu.CompilerParams(dimension_semantics=sem, vmem_limit_bytes=VMEM_LIMIT_BYTES)


def _layer_norm(x, g, b):
    mu = jnp.mean(x, axis=-1, keepdims=True)
    xc = x - mu
    var = jnp.mean(xc * xc, axis=-1, keepdims=True)
    return xc * lax.rsqrt(var + LN_EPS) * g + b


def _dot(a, b):
    return jnp.dot(a, b, preferred_element_type=F32)


def _dot_nt(a, b):
    return lax.dot_general(a, b, (((1,), (1,)), ((), ())), preferred_element_type=F32)


def _dot_tn(a, b):
    return lax.dot_general(a, b, (((0,), (0,)), ((), ())), preferred_element_type=F32)


def _dot_hi_lo(a, b):
    a_hi = a.astype(BF16)
    b_hi = b.astype(BF16)
    a_lo = (a - a_hi.astype(F32)).astype(BF16)
    b_lo = (b - b_hi.astype(F32)).astype(BF16)
    return _dot(a_hi, b_hi) + (_dot(a_lo, b_hi) + _dot(a_hi, b_lo))


def _mm_kernel(a_ref, w_ref, o_ref, *, scale):
    acc = _dot(a_ref[...].astype(BF16), w_ref[...])
    if scale != 1.0:
        acc = acc * scale
    o_ref[...] = acc


def _mm(a, w, scale=1.0, tm=1024, tn=1024, name="mm"):
    M, K = a.shape
    N = w.shape[1]
    tm = min(tm, M)
    tn = math.gcd(tn, N)
    assert M % tm == 0 and N % tn == 0 and tn % LANES == 0, (M, N, tm, tn)
    return pl.pallas_call(
        functools.partial(_mm_kernel, scale=scale),
        grid=(M // tm, N // tn),
        in_specs=[pl.BlockSpec((tm, K), lambda i, j: (i, 0)),
                  pl.BlockSpec((K, tn), lambda i, j: (0, j))],
        out_specs=pl.BlockSpec((tm, tn), lambda i, j: (i, j)),
        out_shape=jax.ShapeDtypeStruct((M, N), F32),
        compiler_params=_params("parallel", "arbitrary"),
        name=name,
    )(a, w)


def _top2_route(logits, n_experts):
    lane = lax.broadcasted_iota(jnp.int32, logits.shape, 1)
    lanef = lane.astype(F32)
    logits = jnp.where(lane < n_experts, logits, -jnp.inf)
    v1 = jnp.max(logits, axis=-1, keepdims=True)
    i1 = jnp.min(jnp.where(logits == v1, lanef, float(LANES)), axis=-1, keepdims=True)
    rest = jnp.where(lanef == i1, -jnp.inf, logits)
    v2 = jnp.max(rest, axis=-1, keepdims=True)
    i2 = jnp.min(jnp.where(rest == v2, lanef, float(LANES)), axis=-1, keepdims=True)
    e2 = jnp.exp(v2 - v1)
    w1 = 1.0 / (1.0 + e2)
    w2 = e2 / (1.0 + e2)
    return jnp.where(lane == 0, i1, jnp.where(lane == 1, i2, jnp.where(lane == 2, w1, w2)))


def _mm_res_ln_kernel(*refs, alpha, want_bf16, n_experts, aliased):
    refs = list(refs)
    a_ref, w_ref, res_ref, g_ref, b_ref = refs[:5]
    del refs[:5]
    if n_experts:
        wr_ref = refs.pop(0)
    if aliased:
        refs.pop(0)
    o_ref = refs.pop(0)
    mix = _dot(a_ref[...], w_ref[...])
    y = _layer_norm(alpha * res_ref[...] + mix, g_ref[...], b_ref[...])
    o_ref[...] = y
    if want_bf16:
        refs.pop(0)[...] = y.astype(BF16)
    if n_experts:
        y_hi = y.astype(BF16)
        y_lo = (y - y_hi.astype(F32)).astype(BF16)
        part = _dot(y_hi, wr_ref[...])
        logits = part[:, 0:LANES] + part[:, LANES:2 * LANES] + _dot(y_lo, wr_ref[:, 0:LANES])
        refs.pop(0)[...] = _top2_route(logits, n_experts)


def _mm_res_ln(a, w, res, g, b, alpha, *, tm=512, want_bf16=True, router_w=None, n_experts=0,
               extra_rows=0, into=None, name="mm_res_ln"):
    M, K = a.shape
    N = w.shape[1]
    tm = min(tm, M)
    assert M % tm == 0 and extra_rows <= tm
    nsteps = M // tm + (1 if extra_rows else 0)
    row = lambda i: (jnp.minimum(i, M // tm - 1), 0)
    fixed = lambda i: (0, 0)
    in_specs = [pl.BlockSpec((tm, K), row), pl.BlockSpec((K, N), fixed),
                pl.BlockSpec((tm, N), row), pl.BlockSpec((1, N), fixed), pl.BlockSpec((1, N), fixed)]
    args = [a, w, res, g.reshape(1, N), b.reshape(1, N)]
    if n_experts:
        in_specs.append(pl.BlockSpec((N, 2 * LANES), fixed))
        args.append(router_w)
    aliases = {}
    rows, off = M + extra_rows, 0
    if into is not None:
        rows = into.shape[0]
        assert (rows - M) % tm == 0
        off = (rows - M) // tm
        aliases = {len(args): 0}
        in_specs.append(pl.BlockSpec(memory_space=pl.ANY))
        args.append(into)
    out_specs = [pl.BlockSpec((tm, N), lambda i: (i + off, 0))]
    out_shape = [jax.ShapeDtypeStruct((rows, N), F32)]
    if want_bf16:
        out_specs.append(pl.BlockSpec((tm, N), row))
        out_shape.append(jax.ShapeDtypeStruct((M, N), BF16))
    if n_experts:
        out_specs.append(pl.BlockSpec((tm, LANES), row))
        out_shape.append(jax.ShapeDtypeStruct((M, LANES), F32))
    outs = pl.pallas_call(
        functools.partial(_mm_res_ln_kernel, alpha=alpha, want_bf16=want_bf16, n_experts=n_experts,
                          aliased=into is not None),
        grid=(nsteps,),
        in_specs=in_specs, out_specs=out_specs, out_shape=out_shape,
        input_output_aliases=aliases,
        compiler_params=_params("arbitrary" if extra_rows else "parallel"),
        name=name,
    )(*args)
    return outs[0] if len(outs) == 1 else outs


def _ffn_kernel(x_ref, wg_ref, wu_ref, wd_ref, res_ref, g_ref, b_ref, *rest, alpha, nf, has_side):
    if has_side:
        side_ref, o_ref, obf_ref, side_out_ref = rest
        side_out_ref[...] = side_ref[...].astype(BF16)
    else:
        o_ref, obf_ref = rest
    f = pl.program_id(1)

    @pl.when(f == 0)
    def _():
        o_ref[...] = jnp.zeros_like(o_ref)

    x = x_ref[...]
    g = _dot(x, wg_ref[...])
    u = _dot(x, wu_ref[...])
    act = (g * jax.nn.sigmoid(g) * u).astype(BF16)
    o_ref[...] += _dot(act, wd_ref[...])

    @pl.when(f == nf - 1)
    def _():
        y = _layer_norm(alpha * res_ref[...] + o_ref[...], g_ref[...], b_ref[...])
        o_ref[...] = y
        obf_ref[...] = y.astype(BF16)


def _ffn(x, w_gu, w_down, res, g, b, alpha, *, tm, tf, side=None, name="ffn"):
    M, D = x.shape
    F = w_down.shape[0]
    assert M % tm == 0 and F % tf == 0
    nf = F // tf
    nsteps = (M // tm) * nf
    row = lambda i, f: (i, 0)
    fixed = lambda i, f: (0, 0)
    in_specs = [
        pl.BlockSpec((tm, D), row),
        pl.BlockSpec((D, tf), lambda i, f: (0, f)),
        pl.BlockSpec((D, tf), lambda i, f: (0, nf + f)),
        pl.BlockSpec((tf, D), lambda i, f: (f, 0)),
        pl.BlockSpec((tm, D), row), pl.BlockSpec((1, D), fixed), pl.BlockSpec((1, D), fixed),
    ]
    args = [x, w_gu, w_gu, w_down, res, g.reshape(1, D), b.reshape(1, D)]
    out_specs = [pl.BlockSpec((tm, D), row), pl.BlockSpec((tm, D), row)]
    out_shape = [jax.ShapeDtypeStruct((M, D), F32), jax.ShapeDtypeStruct((M, D), BF16)]
    if side is not None:
        R, C = side.shape
        cr = _side_chunk_rows(R, nsteps)
        nchunks = R // cr
        chunk = lambda i, f: (jnp.minimum(i * nf + f, nchunks - 1), 0)
        in_specs.append(pl.BlockSpec((cr, C), chunk))
        args.append(side)
        out_specs.append(pl.BlockSpec((cr, C), chunk))
        out_shape.append(jax.ShapeDtypeStruct((R, C), BF16))
    return pl.pallas_call(
        functools.partial(_ffn_kernel, alpha=alpha, nf=nf, has_side=side is not None),
        grid=(M // tm, nf), in_specs=in_specs, out_specs=out_specs, out_shape=out_shape,
        compiler_params=_params("arbitrary" if side is not None else "parallel", "arbitrary"),
        name=name,
    )(*args)


def _moe_ffn_kernel(te_ref, used_ref, src_ref, h_ref, wg_ref, wu_ref, wd_ref, o_ref, xbuf, sem, *, nf, tm):
    i = pl.program_id(0)
    f = pl.program_id(1)
    share = -(-tm // nf)
    slot = i % 2

    def row_copy(src_slot, buf, row):
        return pltpu.make_async_copy(h_ref.at[pl.ds(src_ref[src_slot], 1), :],
                                     xbuf.at[buf, pl.ds(row, 1), :], sem)

    @pl.when(jnp.logical_and(i == 0, f == 0))
    def _():
        def start(r, c):
            row_copy(r, 0, r).start()
            return c

        def wait(r, c):
            row_copy(0, 0, r).wait()
            return c

        lax.fori_loop(0, tm, start, 0, unroll=math.gcd(DMA_UNROLL, tm))
        lax.fori_loop(0, tm, wait, 0, unroll=math.gcd(DMA_UNROLL, tm))

    @pl.when(f == 0)
    def _():
        o_ref[...] = jnp.zeros_like(o_ref)

    @pl.when(used_ref[i] != 0)
    def _():
        copies = [row_copy((i + 1) * (share * nf) + f * share + r, 1 - slot, f * share + r) for r in range(share)]
        for cp in copies:
            cp.start()
        x = xbuf[slot, 0:tm, :].astype(BF16)
        g = _dot(x, wg_ref[0])
        u = _dot(x, wu_ref[0])
        act = (g * jax.nn.sigmoid(g) * u).astype(BF16)
        o_ref[...] += _dot(act, wd_ref[0])
        for cp in copies:
            cp.wait()


def _moe_ffn(h, src, w_gu, w_down, tile_expert, tile_used, *, tm, tf, name="ffn_moe"):
    D = h.shape[1]
    E, F, _ = w_down.shape
    ntiles = tile_expert.shape[0]
    assert src.shape[0] == (ntiles + 1) * tm and F % tf == 0
    nf = F // tf
    stride = -(-tm // nf) * nf
    src = jnp.pad(src.reshape(ntiles + 1, tm), ((0, 0), (0, stride - tm))).reshape(-1)
    buf_rows = -(-stride // 8) * 8
    return pl.pallas_call(
        functools.partial(_moe_ffn_kernel, nf=nf, tm=tm),
        grid_spec=pltpu.PrefetchScalarGridSpec(
            num_scalar_prefetch=3, grid=(ntiles, nf),
            in_specs=[
                pl.BlockSpec(memory_space=pl.ANY),
                pl.BlockSpec((1, D, tf), lambda i, f, te, us, src: (te[i], 0, f * us[i])),
                pl.BlockSpec((1, D, tf), lambda i, f, te, us, src: (te[i], 0, nf + f * us[i])),
                pl.BlockSpec((1, tf, D), lambda i, f, te, us, src: (te[i], f * us[i], 0)),
            ],
            out_specs=pl.BlockSpec((tm, D), lambda i, f, te, us, src: (i, 0)),
            scratch_shapes=[pltpu.VMEM((2, buf_rows, D), F32), pltpu.SemaphoreType.DMA(())]),
        out_shape=jax.ShapeDtypeStruct((ntiles * tm, D), F32),
        compiler_params=_params("arbitrary", "arbitrary"),
        name=name,
    )(tile_expert, tile_used, src, h, w_gu, w_gu, w_down)


def _log_sigmoid(z):
    return jnp.minimum(z, 0.0) - jnp.log(1.0 + jnp.exp(-jnp.abs(z)))


def _gla_prompt_kernel(q_ref, k_ref, v_ref, r_ref, alr_ref, wa_ref, ba_ref, ng_ref, *rest, nchunk, scale, nside):
    side_refs = rest[:nside]
    o_ref, s_ref = rest[nside:nside + 2]
    for src, dst in zip(side_refs, rest[nside + 2:]):
        dst[...] = src[...].astype(BF16)
    C = GLA_CHUNK
    dk = q_ref.shape[1]

    @pl.when(pl.program_id(2) == 0)
    def _():
        s_ref[...] = jnp.zeros_like(s_ref)

    cb = nchunk * C
    dv = v_ref.shape[1]
    z = _dot_hi_lo(alr_ref[...], wa_ref[...]) + ba_ref[...]
    bcum = _log_sigmoid(z) / GLA_TAU
    row_in_chunk = lax.broadcasted_iota(jnp.int32, (cb, dk), 0) % C
    shift = 1
    while shift < C:
        bcum = bcum + jnp.where(row_in_chunk >= shift, pltpu.roll(bcum, shift, 0), 0.0)
        shift *= 2
    bc3 = bcum.reshape(nchunk, C, dk)
    b_last = bc3[:, C - 1:C, :]
    k3 = k_ref[...].reshape(nchunk, C, dk)
    v3 = v_ref[...].astype(BF16).reshape(nchunk, C, dv)
    q_dec = (q_ref[...].reshape(nchunk, C, dk) * scale * jnp.exp(bc3)).astype(BF16)
    k_dec = (k3 * jnp.exp(-bc3)).astype(BF16)
    k_end = (k3 * jnp.exp(b_last - bc3)).astype(BF16)
    causal = lax.broadcasted_iota(jnp.int32, (C, C), 0) >= lax.broadcasted_iota(jnp.int32, (C, C), 1)
    a = jnp.where(causal, jnp.einsum("nid,njd->nij", q_dec, k_dec, preferred_element_type=F32), 0.0)
    o = jnp.einsum("nij,nje->nie", a.astype(BF16), v3, preferred_element_type=F32)
    decay_rows = jnp.concatenate([jnp.exp(b_last).reshape(nchunk, dk), jnp.ones((LANES - nchunk, dk), F32)], axis=0)
    decay_cols = jnp.transpose(decay_rows)
    s = s_ref[0, 0]
    s_starts = []
    for j in range(nchunk):
        s_starts.append(s.astype(BF16))
        s = s * decay_cols[:, j:j + 1] + _dot_tn(k_end[j], v3[j])
    s_ref[0, 0] = s
    o = o + jnp.einsum("nid,nde->nie", q_dec, jnp.stack(s_starts), preferred_element_type=F32)
    o = o.reshape(cb, dv)
    on = o * lax.rsqrt(jnp.mean(o * o, axis=-1, keepdims=True) + LN_EPS) * ng_ref[...]
    rr = r_ref[...]
    o_ref[...] = (on * (rr * jax.nn.sigmoid(rr))).astype(o_ref.dtype)


def _side_cast_specs(sides, nsteps, step_of):
    specs, shapes = [], []
    for s in sides:
        R, C = s.shape
        cr = _side_chunk_rows(R, nsteps)
        index = functools.partial(lambda *g, n: (jnp.minimum(step_of(*g), n - 1), 0), n=R // cr)
        specs.append(pl.BlockSpec((cr, C), index))
        shapes.append(jax.ShapeDtypeStruct((R, C), BF16))
    return specs, shapes


def _gla_prompt(proj, w_a2p, b_a, norm_g, sides=(), *, B, T, H, dk, dv, cb=512):
    N = B * T
    cb = min(cb, T)
    assert T % cb == 0 and cb % GLA_CHUNK == 0
    nc = T // cb
    dkt = H * dk
    dvt = H * dv
    assert (2 * dkt) % dv == 0 and (2 * dkt + 2 * dvt) % LANES == 0
    v0 = (2 * dkt) // dv
    r0 = (2 * dkt + dvt) // dv
    a0 = (2 * dkt + 2 * dvt) // LANES
    rowblk = lambda b, h, c: b * nc + c
    side_specs, side_shapes = _side_cast_specs(sides, B * H * nc, lambda b, h, c: (b * H + h) * nc + c)
    return pl.pallas_call(
        functools.partial(_gla_prompt_kernel, nchunk=cb // GLA_CHUNK, scale=dk ** -0.5, nside=len(sides)),
        grid=(B, H, nc),
        in_specs=[
            pl.BlockSpec((cb, dk), lambda b, h, c: (rowblk(b, h, c), h)),
            pl.BlockSpec((cb, dk), lambda b, h, c: (rowblk(b, h, c), H + h)),
            pl.BlockSpec((cb, dv), lambda b, h, c: (rowblk(b, h, c), v0 + h)),
            pl.BlockSpec((cb, dv), lambda b, h, c: (rowblk(b, h, c), r0 + h)),
            pl.BlockSpec((cb, LANES), lambda b, h, c: (rowblk(b, h, c), a0)),
            pl.BlockSpec((LANES, dk), lambda b, h, c: (0, h)),
            pl.BlockSpec((1, dk), lambda b, h, c: (0, h)),
            pl.BlockSpec((1, dv), lambda b, h, c: (0, 0)),
            *side_specs,
        ],
        out_specs=[
            pl.BlockSpec((cb, dv), lambda b, h, c: (rowblk(b, h, c), h)),
            pl.BlockSpec((1, 1, dk, dv), lambda b, h, c: (b, h, 0, 0)),
            *side_specs,
        ],
        out_shape=[jax.ShapeDtypeStruct((N, dvt), BF16), jax.ShapeDtypeStruct((B, H, dk, dv), F32), *side_shapes],
        compiler_params=_params(*(("arbitrary",) * 3 if sides else ("parallel", "parallel", "arbitrary"))),
        name="gla_prompt",
    )(proj, proj, proj, proj, proj, w_a2p, b_a.reshape(1, dkt), norm_g.reshape(1, dv), *sides)


def _gla_sample_kernel(q_ref, k_ref, v_ref, r_ref, alr_ref, wat_ref, ba_ref, ng_ref, s0_ref, o_ref, s_ref,
                       *, scale, H, dk, dv):
    for h in range(H):
        rows = slice(h * dk, (h + 1) * dk)
        cols = slice(h * dv, (h + 1) * dv)
        z = jnp.sum(wat_ref[rows, :] * alr_ref[0], axis=-1, keepdims=True) + ba_ref[rows, :]
        decay = jnp.exp(_log_sigmoid(z) / GLA_TAU)
        s_new = s0_ref[0, h] * decay + k_ref[0, h] * v_ref[0, :, cols]
        s_ref[0, h] = s_new
        o = jnp.sum((q_ref[0, h] * scale) * s_new, axis=0, keepdims=True)
        on = o * lax.rsqrt(jnp.mean(o * o, axis=-1, keepdims=True) + LN_EPS) * ng_ref[...]
        rr = r_ref[0, :, cols]
        o_ref[0, :, cols] = (on * (rr * jax.nn.sigmoid(rr))).astype(o_ref.dtype)


def _gla_sample(qkvr, alr, w_a2tp, b_a, norm_g, s0, *, H, dk, dv):
    Bd = qkvr.shape[0]
    dkt = H * dk
    dvt = H * dv
    q = qkvr[:, :dkt].reshape(Bd, H, dk, 1)
    k = qkvr[:, dkt:2 * dkt].reshape(Bd, H, dk, 1)
    v = qkvr[:, 2 * dkt:2 * dkt + dvt].reshape(Bd, 1, dvt)
    r = qkvr[:, 2 * dkt + dvt:].reshape(Bd, 1, dvt)
    col = lambda b: (b, 0, 0, 0)
    vec = lambda b: (b, 0, 0)
    fixed = lambda b: (0, 0)
    return pl.pallas_call(
        functools.partial(_gla_sample_kernel, scale=dk ** -0.5, H=H, dk=dk, dv=dv),
        grid=(Bd,),
        in_specs=[
            pl.BlockSpec((1, H, dk, 1), col),
            pl.BlockSpec((1, H, dk, 1), col),
            pl.BlockSpec((1, 1, dvt), vec),
            pl.BlockSpec((1, 1, dvt), vec),
            pl.BlockSpec((1, 1, LANES), vec),
            pl.BlockSpec((dkt, LANES), fixed),
            pl.BlockSpec((dkt, 1), fixed),
            pl.BlockSpec((1, dv), fixed),
            pl.BlockSpec((1, H, dk, dv), col),
        ],
        out_specs=[
            pl.BlockSpec((1, 1, dvt), vec),
            pl.BlockSpec((1, H, dk, dv), col),
        ],
        out_shape=[jax.ShapeDtypeStruct((Bd, 1, dvt), BF16), jax.ShapeDtypeStruct((Bd, H, dk, dv), F32)],
        compiler_params=_params("parallel"),
        name="gla_sample",
    )(q, k, v, r, alr.reshape(Bd, 1, LANES), w_a2tp, b_a.reshape(dkt, 1), norm_g.reshape(1, dv), s0)


def _attn_prompt_kernel(q0_ref, q1_ref, q2_ref, k_ref, v_ref, bvec_ref, side_ref, o_ref, side_out_ref,
                        qb, kb, vb, m_s, l_s, acc_s, *, T, dils):
    side_out_ref[...] = side_ref[...].astype(BF16)
    BQ = ATTN_BLOCK
    q_refs = (q0_ref, q1_ref, q2_ref)
    hd = k_ref.shape[2]
    nblk = T // BQ

    def rows_of(start, n, d):
        return pl.ds(start, n) if d == 1 else pl.ds(start, n, stride=d)

    def softmax_pv(s, v):
        m = jnp.max(s, axis=-1, keepdims=True)
        p = jnp.exp(s - m)
        l = jnp.sum(p, axis=-1, keepdims=True)
        pv = jnp.einsum("bqk,bkd->bqd", p.astype(BF16), v, preferred_element_type=F32)
        return m, l, pv

    for g, d in enumerate(dils):
        nb = T // (d * BQ)
        bias = pltpu.roll(jnp.broadcast_to(bvec_ref[g, 0], (BQ, 3 * BQ)), 0, 1, stride=1, stride_axis=0)
        blocks = [(r, 0) for r in range(d)] + [(r, i) for r in range(d) for i in range(1, nb)]
        nfirst = d
        for e, (r, i) in enumerate(blocks):
            qb[e] = q_refs[g][0, rows_of(r + i * BQ * d, BQ, d), :].astype(BF16)
            if i == 0:
                kb[e, 0:BQ, :] = k_ref[0, rows_of(r, BQ, d), :].astype(BF16)
                vb[e, 0:BQ, :] = v_ref[0, rows_of(r, BQ, d), :].astype(BF16)
            else:
                kb[e] = k_ref[0, rows_of(r + (i - 1) * BQ * d, 2 * BQ, d), :].astype(BF16)
                vb[e] = v_ref[0, rows_of(r + (i - 1) * BQ * d, 2 * BQ, d), :].astype(BF16)
        s_first = jnp.einsum("bqd,bkd->bqk", qb[0:nfirst], kb[0:nfirst, 0:BQ, :],
                             preferred_element_type=F32) + bias[:, BQ:2 * BQ]
        stats = [softmax_pv(s_first, vb[0:nfirst, 0:BQ, :])]
        if nb > 1:
            s_rest = jnp.einsum("bqd,bkd->bqk", qb[nfirst:nblk], kb[nfirst:nblk],
                                preferred_element_type=F32) + bias[:, 0:2 * BQ]
            stats.append(softmax_pv(s_rest, vb[nfirst:nblk]))
        for e, (r, i) in enumerate(blocks):
            m, l, pv = stats[0] if e < nfirst else stats[1]
            idx = e if e < nfirst else e - nfirst
            rows = rows_of(r + i * BQ * d, BQ, d)
            m_s[g, rows, :] = jnp.broadcast_to(m[idx], (BQ, hd))
            l_s[g, rows, :] = jnp.broadcast_to(l[idx], (BQ, hd))
            acc_s[g, rows, :] = pv[idx]

    CH = 2 * BQ
    G = len(dils)
    for c in range(T // CH):
        sl = slice(c * CH, (c + 1) * CH)
        ms = [m_s[g, sl, :] for g in range(G)]
        mm = functools.reduce(jnp.maximum, ms)
        ws = [jnp.exp(m - mm) for m in ms]
        num = functools.reduce(lambda a, b: a + b, [ws[g] * acc_s[g, sl, :] for g in range(G)])
        den = functools.reduce(lambda a, b: a + b, [ws[g] * l_s[g, sl, :] for g in range(G)])
        o_ref[0, sl, :] = (num / den).astype(o_ref.dtype)


def _side_chunk_rows(rows, nsteps):
    return min(c for c in range(16, rows + 1, 16) if rows % c == 0 and rows // c <= nsteps)


def _attn_prompt(q, k, v, bias_vecs, side, *, B, T, H, hd):
    dils = tuple(d for _, d in DSW_GROUPS)
    G = len(dils)
    BQ = ATTN_BLOCK
    for w, d in DSW_GROUPS:
        assert w // d == BQ and T % (d * BQ) == 0
    assert hd == LANES
    nblk = T // BQ
    R, C = side.shape
    cr = _side_chunk_rows(R, B * H)
    nchunks = R // cr
    chunk = lambda b, h: (jnp.minimum(b * H + h, nchunks - 1), 0)
    qspec = lambda g: pl.BlockSpec((1, T, hd), lambda b, h: (b, 0, g * H + h))
    kvspec = pl.BlockSpec((1, T, hd), lambda b, h: (b, 0, h))
    return pl.pallas_call(
        functools.partial(_attn_prompt_kernel, T=T, dils=dils),
        grid=(B, H),
        in_specs=[qspec(0), qspec(1), qspec(2), kvspec, kvspec,
                  pl.BlockSpec((G, 1, 1, 3 * BQ), lambda b, h: (0, h, 0, 0)),
                  pl.BlockSpec((cr, C), chunk)],
        out_specs=[pl.BlockSpec((1, T, hd), lambda b, h: (b, 0, h)), pl.BlockSpec((cr, C), chunk)],
        out_shape=[jax.ShapeDtypeStruct((B, T, H * hd), BF16), jax.ShapeDtypeStruct((R, C), BF16)],
        scratch_shapes=[pltpu.VMEM((nblk, BQ, hd), BF16), pltpu.VMEM((nblk, 2 * BQ, hd), BF16),
                        pltpu.VMEM((nblk, 2 * BQ, hd), BF16),
                        pltpu.VMEM((G, T, hd), F32), pltpu.VMEM((G, T, hd), F32), pltpu.VMEM((G, T, hd), F32)],
        compiler_params=_params("arbitrary", "arbitrary"),
        name="attn_prompt",
    )(q, q, q, k, v, bias_vecs, side)


def _attn_sample_kernel(q_ref, kn_ref, vn_ref, ck0, ck1, ck2, cv0, cv1, cv2, bias_ref, o_ref, *, G):
    cks = (ck0, ck1, ck2)
    cvs = (cv0, cv1, cv2)
    BQ = ATTN_BLOCK
    kn = kn_ref[0]
    vn = vn_ref[0]
    s_cache, s_new = [], []
    for g in range(G):
        qg = q_ref[0, g]
        s_cache.append(jnp.sum(cks[g][0] * qg[None], axis=-1, keepdims=True) + bias_ref[g, 0:BQ])
        s_new.append(jnp.sum(kn * qg, axis=-1, keepdims=True) + bias_ref[g, BQ])
    mx = s_new[0]
    for g in range(G):
        mx = jnp.maximum(mx, jnp.maximum(s_new[g], jnp.max(s_cache[g], axis=0)))
    den = jnp.zeros_like(mx)
    o = jnp.zeros(kn.shape, F32)
    for g in range(G):
        pc = jnp.exp(s_cache[g] - mx)
        pn = jnp.exp(s_new[g] - mx)
        den = den + jnp.sum(pc, axis=0) + pn
        o = o + jnp.sum(pc * cvs[g][0], axis=0) + pn * vn
    o_ref[0] = (o / den).astype(o_ref.dtype)


def _attn_sample(q, k_new, v_new, cache_k, cache_v, bias_rows, *, H, hd):
    Bd, Lc = cache_k.shape[:2]
    G = len(DSW_GROUPS)
    BQ = ATTN_BLOCK
    views_k, views_v, specs = [], [], []
    for w, d in DSW_GROUPS:
        assert w // d == BQ and Lc % (d * BQ) == 0
        views_k.append(cache_k.reshape(Bd, Lc // d, d, H, hd))
        views_v.append(cache_v.reshape(Bd, Lc // d, d, H, hd))
        last = Lc // d // BQ - 1
        specs.append(pl.BlockSpec((1, BQ, None, H, hd),
                                  functools.partial(lambda b, last: (b, last, 0, 0, 0), last=last)))
    one = lambda b: (b, 0, 0)
    return pl.pallas_call(
        functools.partial(_attn_sample_kernel, G=G),
        grid=(Bd,),
        in_specs=[pl.BlockSpec((1, G, H, hd), lambda b: (b, 0, 0, 0)),
                  pl.BlockSpec((1, H, hd), one), pl.BlockSpec((1, H, hd), one),
                  *specs, *specs,
                  pl.BlockSpec(bias_rows.shape, lambda b: (0, 0, 0, 0))],
        out_specs=pl.BlockSpec((1, H, hd), one),
        out_shape=jax.ShapeDtypeStruct((Bd, H, hd), BF16),
        compiler_params=_params("parallel"),
        name="attn_sample",
    )(q.reshape(Bd, G, H, hd), k_new.reshape(Bd, H, hd), v_new.reshape(Bd, H, hd),
      *views_k, *views_v, bias_rows)


def _row_copy(src_ref, dst_ref, sem, src_row, dst_row):
    return pltpu.make_async_copy(src_ref.at[pl.ds(src_row, 1), :], dst_ref.at[pl.ds(dst_row, 1), :], sem)


DMA_UNROLL = 8


def _combine_ln_kernel(p0_ref, p1_ref, rows_ref, w_ref, h_ref, g_ref, b_ref, o_ref, buf0, buf1, sem, *, R, alpha):
    base = pl.program_id(0) * R

    def start(r, c):
        _row_copy(rows_ref, buf0, sem, p0_ref[base + r], r).start()
        _row_copy(rows_ref, buf1, sem, p1_ref[base + r], r).start()
        return c

    lax.fori_loop(0, R, start, 0, unroll=math.gcd(DMA_UNROLL, R))

    def wait(r, c):
        _row_copy(rows_ref, buf0, sem, 0, r).wait()
        _row_copy(rows_ref, buf1, sem, 0, r).wait()
        return c

    lax.fori_loop(0, R, wait, 0, unroll=math.gcd(DMA_UNROLL, R))
    w = w_ref[...]
    moe = w[:, 0:1] * buf0[...] + w[:, 1:2] * buf1[...]
    o_ref[...] = _layer_norm(alpha * h_ref[...] + moe, g_ref[...], b_ref[...])


def _combine_ln(rows, p0, p1, w, h, g, b, alpha, R=256):
    M = p0.shape[0]
    D = h.shape[1]
    R = min(R, M)
    assert M % R == 0 and h.shape[0] >= M
    row = lambda i, p0, p1: (i, 0)
    fixed = lambda i, p0, p1: (0, 0)
    return pl.pallas_call(
        functools.partial(_combine_ln_kernel, R=R, alpha=alpha),
        grid_spec=pltpu.PrefetchScalarGridSpec(
            num_scalar_prefetch=2, grid=(M // R,),
            in_specs=[pl.BlockSpec(memory_space=pl.ANY), pl.BlockSpec((R, TOP_K), row), pl.BlockSpec((R, D), row),
                      pl.BlockSpec((1, D), fixed), pl.BlockSpec((1, D), fixed)],
            out_specs=pl.BlockSpec((R, D), row),
            scratch_shapes=[pltpu.VMEM((R, D), F32), pltpu.VMEM((R, D), F32), pltpu.SemaphoreType.DMA(())]),
        out_shape=jax.ShapeDtypeStruct((M, D), F32),
        compiler_params=_params("arbitrary"),
        name="moe_combine_ln",
    )(p0, p1, rows, w, h, g.reshape(1, D), b.reshape(1, D))


def _routing_tables(top_i, n_experts, tm):
    nt = top_i.shape[0]
    na = TOP_K * nt
    e = top_i.reshape(na)
    onehot = (e[None, :] == jnp.arange(n_experts, dtype=jnp.int32)[:, None]).astype(jnp.int32)
    csum = jnp.cumsum(onehot, axis=1)
    counts = csum[:, -1]
    padded = ((counts + tm - 1) // tm) * tm
    pend = jnp.cumsum(padded)
    pstart = pend - padded
    slot = jnp.sum(onehot * (csum - 1 + pstart[:, None]), axis=0)
    ntiles = (na + n_experts * (tm - 1) + tm - 1) // tm
    src = jnp.zeros(((ntiles + 1) * tm,), jnp.int32).at[slot].set(
        jnp.arange(na, dtype=jnp.int32) // TOP_K, unique_indices=True, indices_are_sorted=False)
    tile_start = jnp.arange(ntiles, dtype=jnp.int32) * tm
    used = (tile_start < pend[-1]).astype(jnp.int32)
    te = jnp.minimum(jnp.sum((tile_start[:, None] >= pend[None, :]).astype(jnp.int32), axis=1), n_experts - 1)
    last_used = jnp.max(jnp.where(used > 0, te, 0))
    te = jnp.where(used > 0, te, last_used)
    slot2 = slot.reshape(nt, TOP_K)
    return src, te, used, slot2[:, 0], slot2[:, 1]


def _t5_bucket_ids(dist, max_dist):
    max_exact = N_BUCKETS // 2
    d = np.asarray(dist, dtype=np.int64)
    log_ratio = np.log(np.maximum(d, max_exact) / max_exact) / math.log(max_dist / max_exact)
    large = np.minimum(max_exact + (log_ratio * (N_BUCKETS - max_exact)).astype(np.int64), N_BUCKETS - 1)
    return np.where(d < max_exact, d, large).astype(np.int32)


def _group_biases(rel_bias, H):
    max_dist = max(w for w, _ in DSW_GROUPS)
    out = []
    for g, (w, d) in enumerate(DSW_GROUPS):
        buckets = _t5_bucket_ids(d * np.arange(w // d + 1), max_dist)
        out.append(rel_bias[buckets][:, g * H:(g + 1) * H].astype(F32))
    return out


def _prompt_bias_vecs(biases):
    BQ = ATTN_BLOCK
    vecs = []
    for bg in biases:
        v = jnp.concatenate([bg[::-1], jnp.full((2 * BQ - 1, bg.shape[1]), NEG_INF, F32)], axis=0)
        vecs.append(jnp.transpose(v)[:, None, :])
    return jnp.stack(vecs)


def _sample_bias_rows(biases):
    BQ = ATTN_BLOCK
    rows = []
    for bg in biases:
        rows.append(jnp.concatenate([bg[::-1][:BQ], bg[0:1], jnp.zeros((7, bg.shape[1]), F32)], axis=0))
    return jnp.stack(rows)[..., None]


def kernel(x_prompt, x_sample, state_gla, cache_k, cache_v, ln_g, ln_b, gla_w_in, gla_w_a2, gla_b_a,
           gla_norm_g, gla_w_out, kv_w, dsw_w_q, dsw_w_out, rel_bias, ffn_w_gu, ffn_w_down,
           moe_w_router, moe_w_gu, moe_w_down):
    B, T, D = x_prompt.shape
    Bd = x_sample.shape[0]
    depth = ln_g.shape[0]
    assert depth == 2 and x_sample.shape[1] == 1
    alpha = (2 * depth) ** 0.25
    rank, dkt = gla_w_a2.shape[1:]
    dv = gla_norm_g.shape[1]
    dvt = gla_w_out.shape[1]
    Hg = dvt // dv
    dk = dkt // Hg
    G = len(DSW_GROUPS)
    Ha = rel_bias.shape[1] // G
    hd = D // Ha
    n_experts = moe_w_router.shape[2]
    f_dense = ffn_w_down.shape[1]
    f_exp = moe_w_down.shape[2]
    N = B * T

    w_in = gla_w_in[0]
    c_a = 2 * dkt + dvt
    w_in_all = jnp.concatenate([w_in[:, :c_a], w_in[:, c_a + rank:],
                                jnp.pad(w_in[:, c_a:c_a + rank], ((0, 0), (0, LANES - rank)))], axis=1).astype(BF16)
    n_in = w_in_all.shape[1]
    tn_in = max(t for t in range(LANES, 1024 + 1, LANES) if n_in % t == 0)
    w_a2p = jnp.pad(gla_w_a2[0], ((0, LANES - rank), (0, 0)))
    w_a2tp = jnp.pad(gla_w_a2[0].T, ((0, 0), (0, LANES - rank)))
    w_gout = gla_w_out[0].astype(BF16)
    w_k = kv_w[:, :D].astype(BF16)
    w_v = kv_w[:, D:].astype(BF16)
    w_q = dsw_w_q[0].astype(BF16)
    w_aout = dsw_w_out[0].astype(BF16)
    w_router = jnp.pad(moe_w_router[0], ((0, 0), (0, LANES - n_experts)))
    w_router_hi = w_router.astype(BF16)
    w_router_p = jnp.concatenate([w_router_hi, (w_router - w_router_hi.astype(F32)).astype(BF16)], axis=1)
    biases = _group_biases(rel_bias, Ha)
    tf_dense = math.gcd(f_dense, 512)
    tm_moe = 512
    tf_exp = max(t for t in range(256, 1024 + 1, 256) if f_exp % t == 0)

    def layer0_in(x2):
        return _mm(x2, w_in_all, tn=tn_in, name="gla_in_proj")

    def layer0_out(o_gla, x2, side=None):
        m = x2.shape[0]
        h1, h1b = _mm_res_ln(o_gla, w_gout, x2, ln_g[0, 0], ln_b[0, 0], alpha, name="gla_out_ln")
        h2, h2b, *side_out = _ffn(h1b, w_ffn_gu, w_ffn_down, h1, ln_g[0, 1], ln_b[0, 1], alpha,
                                  tm=min(512, m), tf=tf_dense, side=side, name="ffn_dense")
        k = _mm(h2b, w_k, name="k_proj")
        v = _mm(h2b, w_v, name="v_proj")
        q = _mm(h2b, w_q, scale=hd ** -0.5, name="q_proj")
        return (h2, k, v, q, *side_out)

    def attn_out(o_att, h2, **kw):
        return _mm_res_ln(o_att, w_aout, h2, ln_g[1, 0], ln_b[1, 0], alpha, want_bf16=False,
                          router_w=w_router_p, n_experts=n_experts, name="attn_out_ln", **kw)

    xp = x_prompt.reshape(N, D)
    proj_p = layer0_in(xp)
    o_gla_p, state_p, w_ffn_gu, w_ffn_down = _gla_prompt(
        proj_p, w_a2p, gla_b_a[0], gla_norm_g[0], sides=(ffn_w_gu[0], ffn_w_down[0]), B=B, T=T, H=Hg, dk=dk, dv=dv)
    h2_p, k_p, v_p, q_p, w_moe_down = layer0_out(o_gla_p, xp, side=moe_w_down[0].reshape(n_experts * f_exp, D))
    w_moe_down = w_moe_down.reshape(n_experts, f_exp, D)
    o_att_p, w_moe_gu = _attn_prompt(q_p.reshape(B, T, G * D), k_p.reshape(B, T, D), v_p.reshape(B, T, D),
                                     _prompt_bias_vecs(biases), moe_w_gu[0].reshape(n_experts * D, 2 * f_exp),
                                     B=B, T=T, H=Ha, hd=hd)
    w_moe_gu = w_moe_gu.reshape(n_experts, D, 2 * f_exp)
    h3, route_p = attn_out(o_att_p.reshape(N, D), h2_p, extra_rows=Bd)

    xs = x_sample.reshape(Bd, D)
    proj_s = layer0_in(xs)
    o_gla_s, state_s = _gla_sample(proj_s[:, :n_in - LANES], proj_s[:, n_in - LANES:], w_a2tp, gla_b_a[0],
                                   gla_norm_g[0], state_gla[0], H=Hg, dk=dk, dv=dv)
    h2_s, k_s, v_s, q_s = layer0_out(o_gla_s.reshape(Bd, dvt), xs)
    o_att_s = _attn_sample(q_s, k_s, v_s, cache_k, cache_v, _sample_bias_rows(biases), H=Ha, hd=hd)
    h3, route_s = attn_out(o_att_s.reshape(Bd, D), h2_s, into=h3)

    route = jnp.concatenate([route_p, route_s], axis=0)
    top_i = route[:, 0:TOP_K].astype(jnp.int32)
    top_w = route[:, TOP_K:2 * TOP_K]
    src, te, used, p0, p1 = _routing_tables(top_i, n_experts, tm_moe)
    y_sorted = _moe_ffn(h3, src, w_moe_gu, w_moe_down, te, used, tm=tm_moe, tf=tf_exp)
    y_p = _combine_ln(y_sorted, p0[:N], p1[:N], top_w[:N], h3, ln_g[1, 1], ln_b[1, 1], alpha)
    y_s = _combine_ln(y_sorted, p0[N:], p1[N:], top_w[N:], h3[N:], ln_g[1, 1], ln_b[1, 1], alpha)

    return (y_p.reshape(B, T, D), y_s.reshape(Bd, 1, D),
            state_p[None], state_s[None],
            k_p.reshape(B, T, Ha, hd), v_p.reshape(B, T, Ha, hd),
            k_s.reshape(Bd, 1, Ha, hd), v_s.reshape(Bd, 1, Ha, hd))
```

```python
import functools
import math

import jax
import jax.numpy as jnp
import numpy as np
from jax import lax
from jax.experimental import pallas as pl
from jax.experimental.pallas import tpu as pltpu

F32 = jnp.float32
BF16 = jnp.bfloat16

GLA_TAU = 16.0
GLA_CHUNK = 64
LN_EPS = 1e-5
NEG_INF = -1e30
DSW_GROUPS = ((128, 1), (512, 4), (2048, 16))
N_BUCKETS = 32
TOP_K = 2

LANES = 128
VMEM_LIMIT_BYTES = 56 * 1024 * 1024
ATTN_BLOCK = 128


def _params(*sem):
    return pltpu.CompilerParams(dimension_semantics=sem, vmem_limit_bytes=VMEM_LIMIT_BYTES)


def _layer_norm(x, g, b):
    mu = jnp.mean(x, axis=-1, keepdims=True)
    xc = x - mu
    var = jnp.mean(xc * xc, axis=-1, keepdims=True)
    return xc * lax.rsqrt(var + LN_EPS) * g + b


def _dot(a, b):
    return jnp.dot(a, b, preferred_element_type=F32)


def _dot_nt(a, b):
    return lax.dot_general(a, b, (((1,), (1,)), ((), ())), preferred_element_type=F32)


def _dot_tn(a, b):
    return lax.dot_general(a, b, (((0,), (0,)), ((), ())), preferred_element_type=F32)


def _dot_hi_lo(a, b):
    a_hi = a.astype(BF16)
    b_hi = b.astype(BF16)
    a_lo = (a - a_hi.astype(F32)).astype(BF16)
    b_lo = (b - b_hi.astype(F32)).astype(BF16)
    return _dot(a_hi, b_hi) + (_dot(a_lo, b_hi) + _dot(a_hi, b_lo))


def _mm_kernel(a_ref, w_ref, o_ref, *, scale):
    acc = _dot(a_ref[...].astype(BF16), w_ref[...])
    if scale != 1.0:
        acc = acc * scale
    o_ref[...] = acc


def _mm(a, w, scale=1.0, tm=1024, tn=1024, name="mm"):
    M, K = a.shape
    N = w.shape[1]
    tm = min(tm, M)
    tn = math.gcd(tn, N)
    assert M % tm == 0 and N % tn == 0 and tn % LANES == 0, (M, N, tm, tn)
    return pl.pallas_call(
        functools.partial(_mm_kernel, scale=scale),
        grid=(M // tm, N // tn),
        in_specs=[pl.BlockSpec((tm, K), lambda i, j: (i, 0)),
                  pl.BlockSpec((K, tn), lambda i, j: (0, j))],
        out_specs=pl.BlockSpec((tm, tn), lambda i, j: (i, j)),
        out_shape=jax.ShapeDtypeStruct((M, N), F32),
        compiler_params=_params("parallel", "arbitrary"),
        name=name,
    )(a, w)


def _top2_route(logits, n_experts):
    lane = lax.broadcasted_iota(jnp.int32, logits.shape, 1)
    lanef = lane.astype(F32)
    logits = jnp.where(lane < n_experts, logits, -jnp.inf)
    v1 = jnp.max(logits, axis=-1, keepdims=True)
    i1 = jnp.min(jnp.where(logits == v1, lanef, float(LANES)), axis=-1, keepdims=True)
    rest = jnp.where(lanef == i1, -jnp.inf, logits)
    v2 = jnp.max(rest, axis=-1, keepdims=True)
    i2 = jnp.min(jnp.where(rest == v2, lanef, float(LANES)), axis=-1, keepdims=True)
    e2 = jnp.exp(v2 - v1)
    w1 = 1.0 / (1.0 + e2)
    w2 = e2 / (1.0 + e2)
    return jnp.where(lane == 0, i1, jnp.where(lane == 1, i2, jnp.where(lane == 2, w1, w2)))


def _mm_res_ln_kernel(*refs, alpha, want_bf16, n_experts, aliased):
    refs = list(refs)
    a_ref, w_ref, res_ref, g_ref, b_ref = refs[:5]
    del refs[:5]
    if n_experts:
        wr_ref = refs.pop(0)
    if aliased:
        refs.pop(0)
    o_ref = refs.pop(0)
    mix = _dot(a_ref[...], w_ref[...])
    y = _layer_norm(alpha * res_ref[...] + mix, g_ref[...], b_ref[...])
    o_ref[...] = y
    if want_bf16:
        refs.pop(0)[...] = y.astype(BF16)
    if n_experts:
        y_hi = y.astype(BF16)
        y_lo = (y - y_hi.astype(F32)).astype(BF16)
        part = _dot(y_hi, wr_ref[...])
        logits = part[:, 0:LANES] + part[:, LANES:2 * LANES] + _dot(y_lo, wr_ref[:, 0:LANES])
        refs.pop(0)[...] = _top2_route(logits, n_experts)


def _mm_res_ln(a, w, res, g, b, alpha, *, tm=512, want_bf16=True, router_w=None, n_experts=0,
               extra_rows=0, into=None, name="mm_res_ln"):
    M, K = a.shape
    N = w.shape[1]
    tm = min(tm, M)
    assert M % tm == 0 and extra_rows <= tm
    nsteps = M // tm + (1 if extra_rows else 0)
    row = lambda i: (jnp.minimum(i, M // tm - 1), 0)
    fixed = lambda i: (0, 0)
    in_specs = [pl.BlockSpec((tm, K), row), pl.BlockSpec((K, N), fixed),
                pl.BlockSpec((tm, N), row), pl.BlockSpec((1, N), fixed), pl.BlockSpec((1, N), fixed)]
    args = [a, w, res, g.reshape(1, N), b.reshape(1, N)]
    if n_experts:
        in_specs.append(pl.BlockSpec((N, 2 * LANES), fixed))
        args.append(router_w)
    aliases = {}
    rows, off = M + extra_rows, 0
    if into is not None:
        rows = into.shape[0]
        assert (rows - M) % tm == 0
        off = (rows - M) // tm
        aliases = {len(args): 0}
        in_specs.append(pl.BlockSpec(memory_space=pl.ANY))
        args.append(into)
    out_specs = [pl.BlockSpec((tm, N), lambda i: (i + off, 0))]
    out_shape = [jax.ShapeDtypeStruct((rows, N), F32)]
    if want_bf16:
        out_specs.append(pl.BlockSpec((tm, N), row))
        out_shape.append(jax.ShapeDtypeStruct((M, N), BF16))
    if n_experts:
        out_specs.append(pl.BlockSpec((tm, LANES), row))
        out_shape.append(jax.ShapeDtypeStruct((M, LANES), F32))
    outs = pl.pallas_call(
        functools.partial(_mm_res_ln_kernel, alpha=alpha, want_bf16=want_bf16, n_experts=n_experts,
                          aliased=into is not None),
        grid=(nsteps,),
        in_specs=in_specs, out_specs=out_specs, out_shape=out_shape,
        input_output_aliases=aliases,
        compiler_params=_params("arbitrary" if extra_rows else "parallel"),
        name=name,
    )(*args)
    return outs[0] if len(outs) == 1 else outs


def _ffn_kernel(x_ref, wg_ref, wu_ref, wd_ref, res_ref, g_ref, b_ref, *rest, alpha, nf, has_side):
    if has_side:
        side_ref, o_ref, obf_ref, side_out_ref = rest
        side_out_ref[...] = side_ref[...].astype(BF16)
    else:
        o_ref, obf_ref = rest
    f = pl.program_id(1)

    @pl.when(f == 0)
    def _():
        o_ref[...] = jnp.zeros_like(o_ref)

    x = x_ref[...]
    g = _dot(x, wg_ref[...])
    u = _dot(x, wu_ref[...])
    act = (g * jax.nn.sigmoid(g) * u).astype(BF16)
    o_ref[...] += _dot(act, wd_ref[...])

    @pl.when(f == nf - 1)
    def _():
        y = _layer_norm(alpha * res_ref[...] + o_ref[...], g_ref[...], b_ref[...])
        o_ref[...] = y
        obf_ref[...] = y.astype(BF16)


def _ffn(x, w_gu, w_down, res, g, b, alpha, *, tm, tf, side=None, name="ffn"):
    M, D = x.shape
    F = w_down.shape[0]
    assert M % tm == 0 and F % tf == 0
    nf = F // tf
    nsteps = (M // tm) * nf
    row = lambda i, f: (i, 0)
    fixed = lambda i, f: (0, 0)
    in_specs = [
        pl.BlockSpec((tm, D), row),
        pl.BlockSpec((D, tf), lambda i, f: (0, f)),
        pl.BlockSpec((D, tf), lambda i, f: (0, nf + f)),
        pl.BlockSpec((tf, D), lambda i, f: (f, 0)),
        pl.BlockSpec((tm, D), row), pl.BlockSpec((1, D), fixed), pl.BlockSpec((1, D), fixed),
    ]
    args = [x, w_gu, w_gu, w_down, res, g.reshape(1, D), b.reshape(1, D)]
    out_specs = [pl.BlockSpec((tm, D), row), pl.BlockSpec((tm, D), row)]
    out_shape = [jax.ShapeDtypeStruct((M, D), F32), jax.ShapeDtypeStruct((M, D), BF16)]
    if side is not None:
        R, C = side.shape
        cr = _side_chunk_rows(R, nsteps)
        nchunks = R // cr
        chunk = lambda i, f: (jnp.minimum(i * nf + f, nchunks - 1), 0)
        in_specs.append(pl.BlockSpec((cr, C), chunk))
        args.append(side)
        out_specs.append(pl.BlockSpec((cr, C), chunk))
        out_shape.append(jax.ShapeDtypeStruct((R, C), BF16))
    return pl.pallas_call(
        functools.partial(_ffn_kernel, alpha=alpha, nf=nf, has_side=side is not None),
        grid=(M // tm, nf), in_specs=in_specs, out_specs=out_specs, out_shape=out_shape,
        compiler_params=_params("arbitrary" if side is not None else "parallel", "arbitrary"),
        name=name,
    )(*args)


def _moe_ffn_kernel(te_ref, used_ref, src_ref, h_ref, wg_ref, wu_ref, wd_ref, o_ref, xbuf, sem, *, nf, tm):
    i = pl.program_id(0)
    f = pl.program_id(1)
    share = -(-tm // nf)
    slot = i % 2

    def row_copy(src_slot, buf, row):
        return pltpu.make_async_copy(h_ref.at[pl.ds(src_ref[src_slot], 1), :],
                                     xbuf.at[buf, pl.ds(row, 1), :], sem)

    @pl.when(jnp.logical_and(i == 0, f == 0))
    def _():
        def start(r, c):
            row_copy(r, 0, r).start()
            return c

        def wait(r, c):
            row_copy(0, 0, r).wait()
            return c

        lax.fori_loop(0, tm, start, 0, unroll=math.gcd(DMA_UNROLL, tm))
        lax.fori_loop(0, tm, wait, 0, unroll=math.gcd(DMA_UNROLL, tm))

    @pl.when(f == 0)
    def _():
        o_ref[...] = jnp.zeros_like(o_ref)

    @pl.when(used_ref[i] != 0)
    def _():
        copies = [row_copy((i + 1) * (share * nf) + f * share + r, 1 - slot, f * share + r) for r in range(share)]
        for cp in copies:
            cp.start()
        x = xbuf[slot, 0:tm, :].astype(BF16)
        g = _dot(x, wg_ref[0])
        u = _dot(x, wu_ref[0])
        act = (g * jax.nn.sigmoid(g) * u).astype(BF16)
        o_ref[...] += _dot(act, wd_ref[0])
        for cp in copies:
            cp.wait()


def _moe_ffn(h, src, w_gu, w_down, tile_expert, tile_used, *, tm, tf, name="ffn_moe"):
    D = h.shape[1]
    E, F, _ = w_down.shape
    ntiles = tile_expert.shape[0]
    assert src.shape[0] == (ntiles + 1) * tm and F % tf == 0
    nf = F // tf
    stride = -(-tm // nf) * nf
    src = jnp.pad(src.reshape(ntiles + 1, tm), ((0, 0), (0, stride - tm))).reshape(-1)
    buf_rows = -(-stride // 8) * 8
    return pl.pallas_call(
        functools.partial(_moe_ffn_kernel, nf=nf, tm=tm),
        grid_spec=pltpu.PrefetchScalarGridSpec(
            num_scalar_prefetch=3, grid=(ntiles, nf),
            in_specs=[
                pl.BlockSpec(memory_space=pl.ANY),
                pl.BlockSpec((1, D, tf), lambda i, f, te, us, src: (te[i], 0, f * us[i])),
                pl.BlockSpec((1, D, tf), lambda i, f, te, us, src: (te[i], 0, nf + f * us[i])),
                pl.BlockSpec((1, tf, D), lambda i, f, te, us, src: (te[i], f * us[i], 0)),
            ],
            out_specs=pl.BlockSpec((tm, D), lambda i, f, te, us, src: (i, 0)),
            scratch_shapes=[pltpu.VMEM((2, buf_rows, D), F32), pltpu.SemaphoreType.DMA(())]),
        out_shape=jax.ShapeDtypeStruct((ntiles * tm, D), F32),
        compiler_params=_params("arbitrary", "arbitrary"),
        name=name,
    )(tile_expert, tile_used, src, h, w_gu, w_gu, w_down)


def _log_sigmoid(z):
    return jnp.minimum(z, 0.0) - jnp.log(1.0 + jnp.exp(-jnp.abs(z)))


def _gla_prompt_kernel(q_ref, k_ref, v_ref, r_ref, alr_ref, wa_ref, ba_ref, ng_ref, *rest, nchunk, scale, nside):
    side_refs = rest[:nside]
    o_ref, s_ref = rest[nside:nside + 2]
    for src, dst in zip(side_refs, rest[nside + 2:]):
        dst[...] = src[...].astype(BF16)
    C = GLA_CHUNK
    dk = q_ref.shape[1]

    @pl.when(pl.program_id(2) == 0)
    def _():
        s_ref[...] = jnp.zeros_like(s_ref)

    cb = nchunk * C
    dv = v_ref.shape[1]
    z = _dot_hi_lo(alr_ref[...], wa_ref[...]) + ba_ref[...]
    bcum = _log_sigmoid(z) / GLA_TAU
    row_in_chunk = lax.broadcasted_iota(jnp.int32, (cb, dk), 0) % C
    shift = 1
    while shift < C:
        bcum = bcum + jnp.where(row_in_chunk >= shift, pltpu.roll(bcum, shift, 0), 0.0)
        shift *= 2
    bc3 = bcum.reshape(nchunk, C, dk)
    b_last = bc3[:, C - 1:C, :]
    k3 = k_ref[...].reshape(nchunk, C, dk)
    v3 = v_ref[...].astype(BF16).reshape(nchunk, C, dv)
    q_dec = (q_ref[...].reshape(nchunk, C, dk) * scale * jnp.exp(bc3)).astype(BF16)
    k_dec = (k3 * jnp.exp(-bc3)).astype(BF16)
    k_end = (k3 * jnp.exp(b_last - bc3)).astype(BF16)
    causal = lax.broadcasted_iota(jnp.int32, (C, C), 0) >= lax.broadcasted_iota(jnp.int32, (C, C), 1)
    a = jnp.where(causal, jnp.einsum("nid,njd->nij", q_dec, k_dec, preferred_element_type=F32), 0.0)
    o = jnp.einsum("nij,nje->nie", a.astype(BF16), v3, preferred_element_type=F32)
    decay_rows = jnp.concatenate([jnp.exp(b_last).reshape(nchunk, dk), jnp.ones((LANES - nchunk, dk), F32)], axis=0)
    decay_cols = jnp.transpose(decay_rows)
    s = s_ref[0, 0]
    s_starts = []
    for j in range(nchunk):
        s_starts.append(s.astype(BF16))
        s = s * decay_cols[:, j:j + 1] + _dot_tn(k_end[j], v3[j])
    s_ref[0, 0] = s
    o = o + jnp.einsum("nid,nde->nie", q_dec, jnp.stack(s_starts), preferred_element_type=F32)
    o = o.reshape(cb, dv)
    on = o * lax.rsqrt(jnp.mean(o * o, axis=-1, keepdims=True) + LN_EPS) * ng_ref[...]
    rr = r_ref[...]
    o_ref[...] = (on * (rr * jax.nn.sigmoid(rr))).astype(o_ref.dtype)


def _side_cast_specs(sides, nsteps, step_of):
    specs, shapes = [], []
    for s in sides:
        R, C = s.shape
        cr = _side_chunk_rows(R, nsteps)
        index = functools.partial(lambda *g, n: (jnp.minimum(step_of(*g), n - 1), 0), n=R // cr)
        specs.append(pl.BlockSpec((cr, C), index))
        shapes.append(jax.ShapeDtypeStruct((R, C), BF16))
    return specs, shapes


def _gla_prompt(proj, w_a2p, b_a, norm_g, sides=(), *, B, T, H, dk, dv, cb=512):
    N = B * T
    cb = min(cb, T)
    assert T % cb == 0 and cb % GLA_CHUNK == 0
    nc = T // cb
    dkt = H * dk
    dvt = H * dv
    assert (2 * dkt) % dv == 0 and (2 * dkt + 2 * dvt) % LANES == 0
    v0 = (2 * dkt) // dv
    r0 = (2 * dkt + dvt) // dv
    a0 = (2 * dkt + 2 * dvt) // LANES
    rowblk = lambda b, h, c: b * nc + c
    side_specs, side_shapes = _side_cast_specs(sides, B * H * nc, lambda b, h, c: (b * H + h) * nc + c)
    return pl.pallas_call(
        functools.partial(_gla_prompt_kernel, nchunk=cb // GLA_CHUNK, scale=dk ** -0.5, nside=len(sides)),
        grid=(B, H, nc),
        in_specs=[
            pl.BlockSpec((cb, dk), lambda b, h, c: (rowblk(b, h, c), h)),
            pl.BlockSpec((cb, dk), lambda b, h, c: (rowblk(b, h, c), H + h)),
            pl.BlockSpec((cb, dv), lambda b, h, c: (rowblk(b, h, c), v0 + h)),
            pl.BlockSpec((cb, dv), lambda b, h, c: (rowblk(b, h, c), r0 + h)),
            pl.BlockSpec((cb, LANES), lambda b, h, c: (rowblk(b, h, c), a0)),
            pl.BlockSpec((LANES, dk), lambda b, h, c: (0, h)),
            pl.BlockSpec((1, dk), lambda b, h, c: (0, h)),
            pl.BlockSpec((1, dv), lambda b, h, c: (0, 0)),
            *side_specs,
        ],
        out_specs=[
            pl.BlockSpec((cb, dv), lambda b, h, c: (rowblk(b, h, c), h)),
            pl.BlockSpec((1, 1, dk, dv), lambda b, h, c: (b, h, 0, 0)),
            *side_specs,
        ],
        out_shape=[jax.ShapeDtypeStruct((N, dvt), BF16), jax.ShapeDtypeStruct((B, H, dk, dv), F32), *side_shapes],
        compiler_params=_params(*(("arbitrary",) * 3 if sides else ("parallel", "parallel", "arbitrary"))),
        name="gla_prompt",
    )(proj, proj, proj, proj, proj, w_a2p, b_a.reshape(1, dkt), norm_g.reshape(1, dv), *sides)


def _gla_sample_kernel(q_ref, k_ref, v_ref, r_ref, alr_ref, wat_ref, ba_ref, ng_ref, s0_ref, o_ref, s_ref,
                       *, scale, H, dk, dv):
    def column(row):
        return jnp.transpose(jnp.broadcast_to(row, (8, dk)))[:, 0:1]

    for h in range(H):
        rows = slice(h * dk, (h + 1) * dk)
        cols = slice(h * dv, (h + 1) * dv)
        z = jnp.sum(wat_ref[rows, :] * alr_ref[0], axis=-1, keepdims=True) + ba_ref[rows, :]
        decay = jnp.exp(_log_sigmoid(z) / GLA_TAU)
        s_new = s0_ref[0, h] * decay + column(k_ref[0, :, rows]) * v_ref[0, :, cols]
        s_ref[0, h] = s_new
        o = jnp.sum((column(q_ref[0, :, rows]) * scale) * s_new, axis=0, keepdims=True)
        on = o * lax.rsqrt(jnp.mean(o * o, axis=-1, keepdims=True) + LN_EPS) * ng_ref[...]
        rr = r_ref[0, :, cols]
        o_ref[0, :, cols] = (on * (rr * jax.nn.sigmoid(rr))).astype(o_ref.dtype)


def _gla_sample(qkvr, alr, w_a2tp, b_a, norm_g, s0, *, H, dk, dv):
    Bd = qkvr.shape[0]
    dkt = H * dk
    dvt = H * dv
    q = qkvr[:, :dkt].reshape(Bd, 1, dkt)
    k = qkvr[:, dkt:2 * dkt].reshape(Bd, 1, dkt)
    v = qkvr[:, 2 * dkt:2 * dkt + dvt].reshape(Bd, 1, dvt)
    r = qkvr[:, 2 * dkt + dvt:].reshape(Bd, 1, dvt)
    col = lambda b: (b, 0, 0, 0)
    vec = lambda b: (b, 0, 0)
    fixed = lambda b: (0, 0)
    return pl.pallas_call(
        functools.partial(_gla_sample_kernel, scale=dk ** -0.5, H=H, dk=dk, dv=dv),
        grid=(Bd,),
        in_specs=[
            pl.BlockSpec((1, 1, dkt), vec),
            pl.BlockSpec((1, 1, dkt), vec),
            pl.BlockSpec((1, 1, dvt), vec),
            pl.BlockSpec((1, 1, dvt), vec),
            pl.BlockSpec((1, 1, LANES), vec),
            pl.BlockSpec((dkt, LANES), fixed),
            pl.BlockSpec((dkt, 1), fixed),
            pl.BlockSpec((1, dv), fixed),
            pl.BlockSpec((1, H, dk, dv), col),
        ],
        out_specs=[
            pl.BlockSpec((1, 1, dvt), vec),
            pl.BlockSpec((1, H, dk, dv), col),
        ],
        out_shape=[jax.ShapeDtypeStruct((Bd, 1, dvt), BF16), jax.ShapeDtypeStruct((Bd, H, dk, dv), F32)],
        compiler_params=_params("parallel"),
        name="gla_sample",
    )(q, k, v, r, alr.reshape(Bd, 1, LANES), w_a2tp, b_a.reshape(dkt, 1), norm_g.reshape(1, dv), s0)


def _attn_prompt_kernel(q0_ref, q1_ref, q2_ref, k_ref, v_ref, bvec_ref, side_ref, o_ref, side_out_ref,
                        qb, kb, vb, m_s, l_s, acc_s, *, T, dils):
    side_out_ref[...] = side_ref[...].astype(BF16)
    BQ = ATTN_BLOCK
    q_refs = (q0_ref, q1_ref, q2_ref)
    hd = k_ref.shape[2]
    nblk = T // BQ

    def rows_of(start, n, d):
        return pl.ds(start, n) if d == 1 else pl.ds(start, n, stride=d)

    def softmax_pv(s, v):
        m = jnp.max(s, axis=-1, keepdims=True)
        p = jnp.exp(s - m)
        l = jnp.sum(p, axis=-1, keepdims=True)
        pv = jnp.einsum("bqk,bkd->bqd", p.astype(BF16), v, preferred_element_type=F32)
        return m, l, pv

    for g, d in enumerate(dils):
        nb = T // (d * BQ)
        bias = pltpu.roll(jnp.broadcast_to(bvec_ref[g, 0], (BQ, 3 * BQ)), 0, 1, stride=1, stride_axis=0)
        blocks = [(r, 0) for r in range(d)] + [(r, i) for r in range(d) for i in range(1, nb)]
        nfirst = d
        for e, (r, i) in enumerate(blocks):
            qb[e] = q_refs[g][0, rows_of(r + i * BQ * d, BQ, d), :].astype(BF16)
            if i == 0:
                kb[e, 0:BQ, :] = k_ref[0, rows_of(r, BQ, d), :].astype(BF16)
                vb[e, 0:BQ, :] = v_ref[0, rows_of(r, BQ, d), :].astype(BF16)
            else:
                kb[e] = k_ref[0, rows_of(r + (i - 1) * BQ * d, 2 * BQ, d), :].astype(BF16)
                vb[e] = v_ref[0, rows_of(r + (i - 1) * BQ * d, 2 * BQ, d), :].astype(BF16)
        s_first = jnp.einsum("bqd,bkd->bqk", qb[0:nfirst], kb[0:nfirst, 0:BQ, :],
                             preferred_element_type=F32) + bias[:, BQ:2 * BQ]
        stats = [softmax_pv(s_first, vb[0:nfirst, 0:BQ, :])]
        if nb > 1:
            s_rest = jnp.einsum("bqd,bkd->bqk", qb[nfirst:nblk], kb[nfirst:nblk],
                                preferred_element_type=F32) + bias[:, 0:2 * BQ]
            stats.append(softmax_pv(s_rest, vb[nfirst:nblk]))
        for e, (r, i) in enumerate(blocks):
            m, l, pv = stats[0] if e < nfirst else stats[1]
            idx = e if e < nfirst else e - nfirst
            rows = rows_of(r + i * BQ * d, BQ, d)
            m_s[g, rows, :] = jnp.broadcast_to(m[idx], (BQ, hd))
            l_s[g, rows, :] = jnp.broadcast_to(l[idx], (BQ, hd))
            acc_s[g, rows, :] = pv[idx]

    CH = 2 * BQ
    G = len(dils)
    for c in range(T // CH):
        sl = slice(c * CH, (c + 1) * CH)
        ms = [m_s[g, sl, :] for g in range(G)]
        mm = functools.reduce(jnp.maximum, ms)
        ws = [jnp.exp(m - mm) for m in ms]
        num = functools.reduce(lambda a, b: a + b, [ws[g] * acc_s[g, sl, :] for g in range(G)])
        den = functools.reduce(lambda a, b: a + b, [ws[g] * l_s[g, sl, :] for g in range(G)])
        o_ref[0, sl, :] = (num / den).astype(o_ref.dtype)


def _side_chunk_rows(rows, nsteps):
    return min(c for c in range(16, rows + 1, 16) if rows % c == 0 and rows // c <= nsteps)


def _attn_prompt(q, k, v, bias_vecs, side, *, B, T, H, hd):
    dils = tuple(d for _, d in DSW_GROUPS)
    G = len(dils)
    BQ = ATTN_BLOCK
    for w, d in DSW_GROUPS:
        assert w // d == BQ and T % (d * BQ) == 0
    assert hd == LANES
    nblk = T // BQ
    R, C = side.shape
    cr = _side_chunk_rows(R, B * H)
    nchunks = R // cr
    chunk = lambda b, h: (jnp.minimum(b * H + h, nchunks - 1), 0)
    qspec = lambda g: pl.BlockSpec((1, T, hd), lambda b, h: (b, 0, g * H + h))
    kvspec = pl.BlockSpec((1, T, hd), lambda b, h: (b, 0, h))
    return pl.pallas_call(
        functools.partial(_attn_prompt_kernel, T=T, dils=dils),
        grid=(B, H),
        in_specs=[qspec(0), qspec(1), qspec(2), kvspec, kvspec,
                  pl.BlockSpec((G, 1, 1, 3 * BQ), lambda b, h: (0, h, 0, 0)),
                  pl.BlockSpec((cr, C), chunk)],
        out_specs=[pl.BlockSpec((1, T, hd), lambda b, h: (b, 0, h)), pl.BlockSpec((cr, C), chunk)],
        out_shape=[jax.ShapeDtypeStruct((B, T, H * hd), BF16), jax.ShapeDtypeStruct((R, C), BF16)],
        scratch_shapes=[pltpu.VMEM((nblk, BQ, hd), BF16), pltpu.VMEM((nblk, 2 * BQ, hd), BF16),
                        pltpu.VMEM((nblk, 2 * BQ, hd), BF16),
                        pltpu.VMEM((G, T, hd), F32), pltpu.VMEM((G, T, hd), F32), pltpu.VMEM((G, T, hd), F32)],
        compiler_params=_params("arbitrary", "arbitrary"),
        name="attn_prompt",
    )(q, q, q, k, v, bias_vecs, side)


def _attn_sample_kernel(q_ref, kn_ref, vn_ref, ck0, ck1, ck2, cv0, cv1, cv2, bias_ref, o_ref, *, G):
    cks = (ck0, ck1, ck2)
    cvs = (cv0, cv1, cv2)
    BQ = ATTN_BLOCK
    kn = kn_ref[0]
    vn = vn_ref[0]
    s_cache, s_new = [], []
    for g in range(G):
        qg = q_ref[0, g]
        s_cache.append(jnp.sum(cks[g][0] * qg[None], axis=-1, keepdims=True) + bias_ref[g, 0:BQ])
        s_new.append(jnp.sum(kn * qg, axis=-1, keepdims=True) + bias_ref[g, BQ])
    mx = s_new[0]
    for g in range(G):
        mx = jnp.maximum(mx, jnp.maximum(s_new[g], jnp.max(s_cache[g], axis=0)))
    den = jnp.zeros_like(mx)
    o = jnp.zeros(kn.shape, F32)
    for g in range(G):
        pc = jnp.exp(s_cache[g] - mx)
        pn = jnp.exp(s_new[g] - mx)
        den = den + jnp.sum(pc, axis=0) + pn
        o = o + jnp.sum(pc * cvs[g][0], axis=0) + pn * vn
    o_ref[0] = (o / den).astype(o_ref.dtype)


def _attn_sample(q, k_new, v_new, cache_k, cache_v, bias_rows, *, H, hd):
    Bd, Lc = cache_k.shape[:2]
    G = len(DSW_GROUPS)
    BQ = ATTN_BLOCK
    views_k, views_v, specs = [], [], []
    for w, d in DSW_GROUPS:
        assert w // d == BQ and Lc % (d * BQ) == 0
        views_k.append(cache_k.reshape(Bd, Lc // d, d, H, hd))
        views_v.append(cache_v.reshape(Bd, Lc // d, d, H, hd))
        last = Lc // d // BQ - 1
        specs.append(pl.BlockSpec((1, BQ, None, H, hd),
                                  functools.partial(lambda b, last: (b, last, 0, 0, 0), last=last)))
    one = lambda b: (b, 0, 0)
    return pl.pallas_call(
        functools.partial(_attn_sample_kernel, G=G),
        grid=(Bd,),
        in_specs=[pl.BlockSpec((1, G, H, hd), lambda b: (b, 0, 0, 0)),
                  pl.BlockSpec((1, H, hd), one), pl.BlockSpec((1, H, hd), one),
                  *specs, *specs,
                  pl.BlockSpec(bias_rows.shape, lambda b: (0, 0, 0, 0))],
        out_specs=pl.BlockSpec((1, H, hd), one),
        out_shape=jax.ShapeDtypeStruct((Bd, H, hd), BF16),
        compiler_params=_params("parallel"),
        name="attn_sample",
    )(q.reshape(Bd, G, H, hd), k_new.reshape(Bd, H, hd), v_new.reshape(Bd, H, hd),
      *views_k, *views_v, bias_rows)


def _row_copy(src_ref, dst_ref, sem, src_row, dst_row):
    return pltpu.make_async_copy(src_ref.at[pl.ds(src_row, 1), :], dst_ref.at[pl.ds(dst_row, 1), :], sem)


DMA_UNROLL = 8


def _combine_ln_kernel(p0_ref, p1_ref, rows_ref, w_ref, h_ref, g_ref, b_ref, o_ref, buf, sem, *, R, alpha, nsteps):
    i = pl.program_id(0)
    slot = i % 2
    unroll = math.gcd(DMA_UNROLL, R)

    def fetch(step, sl):
        def start(r, c):
            _row_copy(rows_ref, buf.at[sl, 0], sem.at[sl], p0_ref[step * R + r], r).start()
            _row_copy(rows_ref, buf.at[sl, 1], sem.at[sl], p1_ref[step * R + r], r).start()
            return c

        lax.fori_loop(0, R, start, 0, unroll=unroll)

    @pl.when(i == 0)
    def _():
        fetch(0, 0)

    @pl.when(i + 1 < nsteps)
    def _():
        fetch(i + 1, 1 - slot)

    def wait(r, c):
        _row_copy(rows_ref, buf.at[slot, 0], sem.at[slot], 0, r).wait()
        _row_copy(rows_ref, buf.at[slot, 1], sem.at[slot], 0, r).wait()
        return c

    lax.fori_loop(0, R, wait, 0, unroll=unroll)
    w = w_ref[...]
    moe = w[:, 0:1] * buf[slot, 0] + w[:, 1:2] * buf[slot, 1]
    o_ref[...] = _layer_norm(alpha * h_ref[...] + moe, g_ref[...], b_ref[...])


def _combine_ln(rows, p0, p1, w, h, g, b, alpha, R=256):
    M = p0.shape[0]
    D = h.shape[1]
    R = min(R, M)
    assert M % R == 0 and h.shape[0] >= M
    row = lambda i, p0, p1: (i, 0)
    fixed = lambda i, p0, p1: (0, 0)
    return pl.pallas_call(
        functools.partial(_combine_ln_kernel, R=R, alpha=alpha, nsteps=M // R),
        grid_spec=pltpu.PrefetchScalarGridSpec(
            num_scalar_prefetch=2, grid=(M // R,),
            in_specs=[pl.BlockSpec(memory_space=pl.ANY), pl.BlockSpec((R, TOP_K), row), pl.BlockSpec((R, D), row),
                      pl.BlockSpec((1, D), fixed), pl.BlockSpec((1, D), fixed)],
            out_specs=pl.BlockSpec((R, D), row),
            scratch_shapes=[pltpu.VMEM((2, TOP_K, R, D), F32), pltpu.SemaphoreType.DMA((2,))]),
        out_shape=jax.ShapeDtypeStruct((M, D), F32),
        compiler_params=_params("arbitrary"),
        name="moe_combine_ln",
    )(p0, p1, rows, w, h, g.reshape(1, D), b.reshape(1, D))


def _slot_sources_kernel(slot_ref, o_ref, *, na, ns):
    def clear(s, c):
        o_ref[s] = 0
        return c

    lax.fori_loop(0, ns, clear, 0, unroll=math.gcd(DMA_UNROLL, ns))

    def place(a, c):
        o_ref[slot_ref[a]] = a // TOP_K
        return c

    lax.fori_loop(0, na, place, 0, unroll=math.gcd(DMA_UNROLL, na))


def _slot_sources(slot, ns):
    na = slot.shape[0]
    return pl.pallas_call(
        functools.partial(_slot_sources_kernel, na=na, ns=ns),
        in_specs=[pl.BlockSpec(memory_space=pltpu.SMEM)],
        out_specs=pl.BlockSpec(memory_space=pltpu.SMEM),
        out_shape=jax.ShapeDtypeStruct((ns,), jnp.int32),
        name="moe_slot_sources",
    )(slot)


def _routing_tables(top_i, n_experts, tm):
    nt = top_i.shape[0]
    na = TOP_K * nt
    e = top_i.reshape(na)
    onehot = (e[None, :] == jnp.arange(n_experts, dtype=jnp.int32)[:, None]).astype(jnp.int32)
    csum = jnp.cumsum(onehot, axis=1)
    counts = csum[:, -1]
    padded = ((counts + tm - 1) // tm) * tm
    pend = jnp.cumsum(padded)
    pstart = pend - padded
    slot = jnp.sum(onehot * (csum - 1 + pstart[:, None]), axis=0)
    ntiles = (na + n_experts * (tm - 1) + tm - 1) // tm
    src = _slot_sources(slot, (ntiles + 1) * tm)
    tile_start = jnp.arange(ntiles, dtype=jnp.int32) * tm
    used = (tile_start < pend[-1]).astype(jnp.int32)
    te = jnp.minimum(jnp.sum((tile_start[:, None] >= pend[None, :]).astype(jnp.int32), axis=1), n_experts - 1)
    last_used = jnp.max(jnp.where(used > 0, te, 0))
    te = jnp.where(used > 0, te, last_used)
    slot2 = slot.reshape(nt, TOP_K)
    return src, te, used, slot2[:, 0], slot2[:, 1]


def _t5_bucket_ids(dist, max_dist):
    max_exact = N_BUCKETS // 2
    d = np.asarray(dist, dtype=np.int64)
    log_ratio = np.log(np.maximum(d, max_exact) / max_exact) / math.log(max_dist / max_exact)
    large = np.minimum(max_exact + (log_ratio * (N_BUCKETS - max_exact)).astype(np.int64), N_BUCKETS - 1)
    return np.where(d < max_exact, d, large).astype(np.int32)


def _group_biases(rel_bias, H):
    max_dist = max(w for w, _ in DSW_GROUPS)
    out = []
    for g, (w, d) in enumerate(DSW_GROUPS):
        buckets = _t5_bucket_ids(d * np.arange(w // d + 1), max_dist)
        out.append(rel_bias[buckets][:, g * H:(g + 1) * H].astype(F32))
    return out


def _prompt_bias_vecs(biases):
    BQ = ATTN_BLOCK
    vecs = []
    for bg in biases:
        v = jnp.concatenate([bg[::-1], jnp.full((2 * BQ - 1, bg.shape[1]), NEG_INF, F32)], axis=0)
        vecs.append(jnp.transpose(v)[:, None, :])
    return jnp.stack(vecs)


def _sample_bias_rows(biases):
    BQ = ATTN_BLOCK
    rows = []
    for bg in biases:
        rows.append(jnp.concatenate([bg[::-1][:BQ], bg[0:1], jnp.zeros((7, bg.shape[1]), F32)], axis=0))
    return jnp.stack(rows)[..., None]


def kernel(x_prompt, x_sample, state_gla, cache_k, cache_v, ln_g, ln_b, gla_w_in, gla_w_a2, gla_b_a,
           gla_norm_g, gla_w_out, kv_w, dsw_w_q, dsw_w_out, rel_bias, ffn_w_gu, ffn_w_down,
           moe_w_router, moe_w_gu, moe_w_down):
    B, T, D = x_prompt.shape
    Bd = x_sample.shape[0]
    depth = ln_g.shape[0]
    assert depth == 2 and x_sample.shape[1] == 1
    alpha = (2 * depth) ** 0.25
    rank, dkt = gla_w_a2.shape[1:]
    dv = gla_norm_g.shape[1]
    dvt = gla_w_out.shape[1]
    Hg = dvt // dv
    dk = dkt // Hg
    G = len(DSW_GROUPS)
    Ha = rel_bias.shape[1] // G
    hd = D // Ha
    n_experts = moe_w_router.shape[2]
    f_dense = ffn_w_down.shape[1]
    f_exp = moe_w_down.shape[2]
    N = B * T

    w_in = gla_w_in[0]
    c_a = 2 * dkt + dvt
    w_in_all = jnp.concatenate([w_in[:, :c_a], w_in[:, c_a + rank:],
                                jnp.pad(w_in[:, c_a:c_a + rank], ((0, 0), (0, LANES - rank)))], axis=1).astype(BF16)
    n_in = w_in_all.shape[1]
    tn_in = max(t for t in range(LANES, 1024 + 1, LANES) if n_in % t == 0)
    w_a2p = jnp.pad(gla_w_a2[0], ((0, LANES - rank), (0, 0)))
    w_a2tp = jnp.pad(gla_w_a2[0].T, ((0, 0), (0, LANES - rank)))
    w_gout = gla_w_out[0].astype(BF16)
    w_k = kv_w[:, :D].astype(BF16)
    w_v = kv_w[:, D:].astype(BF16)
    w_q = dsw_w_q[0].astype(BF16)
    w_aout = dsw_w_out[0].astype(BF16)
    w_router = jnp.pad(moe_w_router[0], ((0, 0), (0, LANES - n_experts)))
    w_router_hi = w_router.astype(BF16)
    w_router_p = jnp.concatenate([w_router_hi, (w_router - w_router_hi.astype(F32)).astype(BF16)], axis=1)
    biases = _group_biases(rel_bias, Ha)
    tf_dense = math.gcd(f_dense, 512)
    tm_moe = 512
    tf_exp = max(t for t in range(256, 1024 + 1, 256) if f_exp % t == 0)

    def layer0_in(x2):
        return _mm(x2, w_in_all, tn=tn_in, name="gla_in_proj")

    def layer0_out(o_gla, x2, side=None):
        m = x2.shape[0]
        h1, h1b = _mm_res_ln(o_gla, w_gout, x2, ln_g[0, 0], ln_b[0, 0], alpha, name="gla_out_ln")
        h2, h2b, *side_out = _ffn(h1b, w_ffn_gu, w_ffn_down, h1, ln_g[0, 1], ln_b[0, 1], alpha,
                                  tm=min(512, m), tf=tf_dense, side=side, name="ffn_dense")
        k = _mm(h2b, w_k, name="k_proj")
        v = _mm(h2b, w_v, name="v_proj")
        q = _mm(h2b, w_q, scale=hd ** -0.5, name="q_proj")
        return (h2, k, v, q, *side_out)

    def attn_out(o_att, h2, **kw):
        return _mm_res_ln(o_att, w_aout, h2, ln_g[1, 0], ln_b[1, 0], alpha, want_bf16=False,
                          router_w=w_router_p, n_experts=n_experts, name="attn_out_ln", **kw)

    xp = x_prompt.reshape(N, D)
    proj_p = layer0_in(xp)
    o_gla_p, state_p, w_ffn_gu, w_ffn_down = _gla_prompt(
        proj_p, w_a2p, gla_b_a[0], gla_norm_g[0], sides=(ffn_w_gu[0], ffn_w_down[0]), B=B, T=T, H=Hg, dk=dk, dv=dv)
    h2_p, k_p, v_p, q_p, w_moe_down = layer0_out(o_gla_p, xp, side=moe_w_down[0].reshape(n_experts * f_exp, D))
    w_moe_down = w_moe_down.reshape(n_experts, f_exp, D)
    o_att_p, w_moe_gu = _attn_prompt(q_p.reshape(B, T, G * D), k_p.reshape(B, T, D), v_p.reshape(B, T, D),
                                     _prompt_bias_vecs(biases), moe_w_gu[0].reshape(n_experts * D, 2 * f_exp),
                                     B=B, T=T, H=Ha, hd=hd)
    w_moe_gu = w_moe_gu.reshape(n_experts, D, 2 * f_exp)
    h3, route_p = attn_out(o_att_p.reshape(N, D), h2_p, extra_rows=Bd)

    xs = x_sample.reshape(Bd, D)
    proj_s = layer0_in(xs)
    o_gla_s, state_s = _gla_sample(proj_s[:, :n_in - LANES], proj_s[:, n_in - LANES:], w_a2tp, gla_b_a[0],
                                   gla_norm_g[0], state_gla[0], H=Hg, dk=dk, dv=dv)
    h2_s, k_s, v_s, q_s = layer0_out(o_gla_s.reshape(Bd, dvt), xs)
    o_att_s = _attn_sample(q_s, k_s, v_s, cache_k, cache_v, _sample_bias_rows(biases), H=Ha, hd=hd)
    h3, route_s = attn_out(o_att_s.reshape(Bd, D), h2_s, into=h3)

    route = jnp.concatenate([route_p, route_s], axis=0)
    top_i = route[:, 0:TOP_K].astype(jnp.int32)
    top_w = route[:, TOP_K:2 * TOP_K]
    src, te, used, p0, p1 = _routing_tables(top_i, n_experts, tm_moe)
    y_sorted = _moe_ffn(h3, src, w_moe_gu, w_moe_down, te, used, tm=tm_moe, tf=tf_exp)
    y_p = _combine_ln(y_sorted, p0[:N], p1[:N], top_w[:N], h3, ln_g[1, 1], ln_b[1, 1], alpha)
    y_s = _combine_ln(y_sorted, p0[N:], p1[N:], top_w[N:], h3[N:], ln_g[1, 1], ln_b[1, 1], alpha)

    return (y_p.reshape(B, T, D), y_s.reshape(Bd, 1, D),
            state_p[None], state_s[None],
            k_p.reshape(B, T, Ha, hd), v_p.reshape(B, T, Ha, hd),
            k_s.reshape(Bd, 1, Ha, hd), v_s.reshape(Bd, 1, Ha, hd))
```

```python
import functools
import math

import jax
import jax.numpy as jnp
import numpy as np
from jax import lax
from jax.experimental import pallas as pl
from jax.experimental.pallas import tpu as pltpu

F32 = jnp.float32
BF16 = jnp.bfloat16

GLA_TAU = 16.0
GLA_CHUNK = 64
LN_EPS = 1e-5
NEG_INF = -1e30
DSW_GROUPS = ((128, 1), (512, 4), (2048, 16))
N_BUCKETS = 32
TOP_K = 2

LANES = 128
VMEM_LIMIT_BYTES = 56 * 1024 * 1024
ATTN_BLOCK = 128


def _params(*sem):
    return pltpu.CompilerParams(dimension_semantics=sem, vmem_limit_bytes=VMEM_LIMIT_BYTES)


def _layer_norm(x, g, b):
    mu = jnp.mean(x, axis=-1, keepdims=True)
    xc = x - mu
    var = jnp.mean(xc * xc, axis=-1, keepdims=True)
    return xc * lax.rsqrt(var + LN_EPS) * g + b


def _dot(a, b):
    return jnp.dot(a, b, preferred_element_type=F32)


def _dot_nt(a, b):
    return lax.dot_general(a, b, (((1,), (1,)), ((), ())), preferred_element_type=F32)


def _dot_tn(a, b):
    return lax.dot_general(a, b, (((0,), (0,)), ((), ())), preferred_element_type=F32)


def _dot_hi_lo(a, b):
    a_hi = a.astype(BF16)
    b_hi = b.astype(BF16)
    a_lo = (a - a_hi.astype(F32)).astype(BF16)
    b_lo = (b - b_hi.astype(F32)).astype(BF16)
    return _dot(a_hi, b_hi) + (_dot(a_lo, b_hi) + _dot(a_hi, b_lo))


def _mm_kernel(a_ref, w_ref, o_ref, *, scale):
    acc = _dot(a_ref[...].astype(BF16), w_ref[...])
    if scale != 1.0:
        acc = acc * scale
    o_ref[...] = acc


def _mm(a, w, scale=1.0, tm=1024, tn=1024, name="mm"):
    M, K = a.shape
    N = w.shape[1]
    tm = min(tm, M)
    tn = math.gcd(tn, N)
    assert M % tm == 0 and N % tn == 0 and tn % LANES == 0, (M, N, tm, tn)
    return pl.pallas_call(
        functools.partial(_mm_kernel, scale=scale),
        grid=(M // tm, N // tn),
        in_specs=[pl.BlockSpec((tm, K), lambda i, j: (i, 0)),
                  pl.BlockSpec((K, tn), lambda i, j: (0, j))],
        out_specs=pl.BlockSpec((tm, tn), lambda i, j: (i, j)),
        out_shape=jax.ShapeDtypeStruct((M, N), F32),
        compiler_params=_params("parallel", "arbitrary"),
        name=name,
    )(a, w)


def _top2_route(logits, n_experts):
    lane = lax.broadcasted_iota(jnp.int32, logits.shape, 1)
    lanef = lane.astype(F32)
    logits = jnp.where(lane < n_experts, logits, -jnp.inf)
    v1 = jnp.max(logits, axis=-1, keepdims=True)
    i1 = jnp.min(jnp.where(logits == v1, lanef, float(LANES)), axis=-1, keepdims=True)
    rest = jnp.where(lanef == i1, -jnp.inf, logits)
    v2 = jnp.max(rest, axis=-1, keepdims=True)
    i2 = jnp.min(jnp.where(rest == v2, lanef, float(LANES)), axis=-1, keepdims=True)
    e2 = jnp.exp(v2 - v1)
    w1 = 1.0 / (1.0 + e2)
    w2 = e2 / (1.0 + e2)
    return jnp.where(lane == 0, i1, jnp.where(lane == 1, i2, jnp.where(lane == 2, w1, w2)))


def _mm_res_ln_kernel(*refs, alpha, want_bf16, n_experts, aliased):
    refs = list(refs)
    a_ref, w_ref, res_ref, g_ref, b_ref = refs[:5]
    del refs[:5]
    if n_experts:
        wr_ref = refs.pop(0)
    if aliased:
        refs.pop(0)
    o_ref = refs.pop(0)
    mix = _dot(a_ref[...], w_ref[...])
    y = _layer_norm(alpha * res_ref[...] + mix, g_ref[...], b_ref[...])
    o_ref[...] = y
    if want_bf16:
        refs.pop(0)[...] = y.astype(BF16)
    if n_experts:
        y_hi = y.astype(BF16)
        y_lo = (y - y_hi.astype(F32)).astype(BF16)
        part = _dot(y_hi, wr_ref[...])
        logits = part[:, 0:LANES] + part[:, LANES:2 * LANES] + _dot(y_lo, wr_ref[:, 0:LANES])
        refs.pop(0)[...] = _top2_route(logits, n_experts)


def _mm_res_ln(a, w, res, g, b, alpha, *, tm=512, want_bf16=True, router_w=None, n_experts=0,
               extra_rows=0, into=None, name="mm_res_ln"):
    M, K = a.shape
    N = w.shape[1]
    tm = min(tm, M)
    assert M % tm == 0 and extra_rows <= tm
    nsteps = M // tm + (1 if extra_rows else 0)
    row = lambda i: (jnp.minimum(i, M // tm - 1), 0)
    fixed = lambda i: (0, 0)
    in_specs = [pl.BlockSpec((tm, K), row), pl.BlockSpec((K, N), fixed),
                pl.BlockSpec((tm, N), row), pl.BlockSpec((1, N), fixed), pl.BlockSpec((1, N), fixed)]
    args = [a, w, res, g.reshape(1, N), b.reshape(1, N)]
    if n_experts:
        in_specs.append(pl.BlockSpec((N, 2 * LANES), fixed))
        args.append(router_w)
    aliases = {}
    rows, off = M + extra_rows, 0
    if into is not None:
        rows = into.shape[0]
        assert (rows - M) % tm == 0
        off = (rows - M) // tm
        aliases = {len(args): 0}
        in_specs.append(pl.BlockSpec(memory_space=pl.ANY))
        args.append(into)
    out_specs = [pl.BlockSpec((tm, N), lambda i: (i + off, 0))]
    out_shape = [jax.ShapeDtypeStruct((rows, N), F32)]
    if want_bf16:
        out_specs.append(pl.BlockSpec((tm, N), row))
        out_shape.append(jax.ShapeDtypeStruct((M, N), BF16))
    if n_experts:
        out_specs.append(pl.BlockSpec((tm, LANES), row))
        out_shape.append(jax.ShapeDtypeStruct((M, LANES), F32))
    outs = pl.pallas_call(
        functools.partial(_mm_res_ln_kernel, alpha=alpha, want_bf16=want_bf16, n_experts=n_experts,
                          aliased=into is not None),
        grid=(nsteps,),
        in_specs=in_specs, out_specs=out_specs, out_shape=out_shape,
        input_output_aliases=aliases,
        compiler_params=_params("arbitrary" if extra_rows else "parallel"),
        name=name,
    )(*args)
    return outs[0] if len(outs) == 1 else outs


def _ffn_kernel(x_ref, wg_ref, wu_ref, wd_ref, res_ref, g_ref, b_ref, *rest, alpha, nf, has_side):
    if has_side:
        side_ref, o_ref, obf_ref, side_out_ref = rest
        side_out_ref[...] = side_ref[...].astype(BF16)
    else:
        o_ref, obf_ref = rest
    f = pl.program_id(1)

    @pl.when(f == 0)
    def _():
        o_ref[...] = jnp.zeros_like(o_ref)

    x = x_ref[...]
    g = _dot(x, wg_ref[...])
    u = _dot(x, wu_ref[...])
    act = (g * jax.nn.sigmoid(g) * u).astype(BF16)
    o_ref[...] += _dot(act, wd_ref[...])

    @pl.when(f == nf - 1)
    def _():
        y = _layer_norm(alpha * res_ref[...] + o_ref[...], g_ref[...], b_ref[...])
        o_ref[...] = y
        obf_ref[...] = y.astype(BF16)


def _ffn(x, w_gu, w_down, res, g, b, alpha, *, tm, tf, side=None, name="ffn"):
    M, D = x.shape
    F = w_down.shape[0]
    assert M % tm == 0 and F % tf == 0
    nf = F // tf
    nsteps = (M // tm) * nf
    row = lambda i, f: (i, 0)
    fixed = lambda i, f: (0, 0)
    in_specs = [
        pl.BlockSpec((tm, D), row),
        pl.BlockSpec((D, tf), lambda i, f: (0, f)),
        pl.BlockSpec((D, tf), lambda i, f: (0, nf + f)),
        pl.BlockSpec((tf, D), lambda i, f: (f, 0)),
        pl.BlockSpec((tm, D), row), pl.BlockSpec((1, D), fixed), pl.BlockSpec((1, D), fixed),
    ]
    args = [x, w_gu, w_gu, w_down, res, g.reshape(1, D), b.reshape(1, D)]
    out_specs = [pl.BlockSpec((tm, D), row), pl.BlockSpec((tm, D), row)]
    out_shape = [jax.ShapeDtypeStruct((M, D), F32), jax.ShapeDtypeStruct((M, D), BF16)]
    if side is not None:
        R, C = side.shape
        cr = _side_chunk_rows(R, nsteps)
        nchunks = R // cr
        chunk = lambda i, f: (jnp.minimum(i * nf + f, nchunks - 1), 0)
        in_specs.append(pl.BlockSpec((cr, C), chunk))
        args.append(side)
        out_specs.append(pl.BlockSpec((cr, C), chunk))
        out_shape.append(jax.ShapeDtypeStruct((R, C), BF16))
    return pl.pallas_call(
        functools.partial(_ffn_kernel, alpha=alpha, nf=nf, has_side=side is not None),
        grid=(M // tm, nf), in_specs=in_specs, out_specs=out_specs, out_shape=out_shape,
        compiler_params=_params("arbitrary" if side is not None else "parallel", "arbitrary"),
        name=name,
    )(*args)


def _moe_ffn_kernel(te_ref, used_ref, src_ref, h_ref, wg_ref, wu_ref, wd_ref, o_ref, xbuf, sem, *, nf, tm):
    i = pl.program_id(0)
    f = pl.program_id(1)
    share = -(-tm // nf)
    slot = i % 2

    def row_copy(src_slot, buf, row):
        return pltpu.make_async_copy(h_ref.at[pl.ds(src_ref[src_slot], 1), :],
                                     xbuf.at[buf, pl.ds(row, 1), :], sem)

    @pl.when(jnp.logical_and(i == 0, f == 0))
    def _():
        def start(r, c):
            row_copy(r, 0, r).start()
            return c

        def wait(r, c):
            row_copy(0, 0, r).wait()
            return c

        lax.fori_loop(0, tm, start, 0, unroll=math.gcd(DMA_UNROLL, tm))
        lax.fori_loop(0, tm, wait, 0, unroll=math.gcd(DMA_UNROLL, tm))

    @pl.when(f == 0)
    def _():
        o_ref[...] = jnp.zeros_like(o_ref)

    @pl.when(used_ref[i] != 0)
    def _():
        copies = [row_copy((i + 1) * (share * nf) + f * share + r, 1 - slot, f * share + r) for r in range(share)]
        for cp in copies:
            cp.start()
        x = xbuf[slot, 0:tm, :].astype(BF16)
        g = _dot(x, wg_ref[0])
        u = _dot(x, wu_ref[0])
        act = (g * jax.nn.sigmoid(g) * u).astype(BF16)
        o_ref[...] += _dot(act, wd_ref[0])
        for cp in copies:
            cp.wait()


def _moe_ffn(h, src, w_gu, w_down, tile_expert, tile_used, *, tm, tf, name="ffn_moe"):
    D = h.shape[1]
    E, F, _ = w_down.shape
    ntiles = tile_expert.shape[0]
    assert src.shape[0] == (ntiles + 1) * tm and F % tf == 0
    nf = F // tf
    stride = -(-tm // nf) * nf
    src = jnp.pad(src.reshape(ntiles + 1, tm), ((0, 0), (0, stride - tm))).reshape(-1)
    buf_rows = -(-stride // 8) * 8
    return pl.pallas_call(
        functools.partial(_moe_ffn_kernel, nf=nf, tm=tm),
        grid_spec=pltpu.PrefetchScalarGridSpec(
            num_scalar_prefetch=3, grid=(ntiles, nf),
            in_specs=[
                pl.BlockSpec(memory_space=pl.ANY),
                pl.BlockSpec((1, D, tf), lambda i, f, te, us, src: (te[i], 0, f * us[i])),
                pl.BlockSpec((1, D, tf), lambda i, f, te, us, src: (te[i], 0, nf + f * us[i])),
                pl.BlockSpec((1, tf, D), lambda i, f, te, us, src: (te[i], f * us[i], 0)),
            ],
            out_specs=pl.BlockSpec((tm, D), lambda i, f, te, us, src: (i, 0)),
            scratch_shapes=[pltpu.VMEM((2, buf_rows, D), F32), pltpu.SemaphoreType.DMA(())]),
        out_shape=jax.ShapeDtypeStruct((ntiles * tm, D), F32),
        compiler_params=_params("arbitrary", "arbitrary"),
        name=name,
    )(tile_expert, tile_used, src, h, w_gu, w_gu, w_down)


def _log_sigmoid(z):
    return jnp.minimum(z, 0.0) - jnp.log(1.0 + jnp.exp(-jnp.abs(z)))


def _gla_prompt_kernel(q_ref, k_ref, v_ref, r_ref, alr_ref, wa_ref, ba_ref, ng_ref, *rest, nchunk, scale, nside):
    side_refs = rest[:nside]
    o_ref, s_ref = rest[nside:nside + 2]
    for src, dst in zip(side_refs, rest[nside + 2:]):
        dst[...] = src[...].astype(BF16)
    C = GLA_CHUNK
    dk = q_ref.shape[1]

    @pl.when(pl.program_id(2) == 0)
    def _():
        s_ref[...] = jnp.zeros_like(s_ref)

    cb = nchunk * C
    dv = v_ref.shape[1]
    z = _dot_hi_lo(alr_ref[...], wa_ref[...]) + ba_ref[...]
    bcum = _log_sigmoid(z) / GLA_TAU
    row_in_chunk = lax.broadcasted_iota(jnp.int32, (cb, dk), 0) % C
    shift = 1
    while shift < C:
        bcum = bcum + jnp.where(row_in_chunk >= shift, pltpu.roll(bcum, shift, 0), 0.0)
        shift *= 2
    bc3 = bcum.reshape(nchunk, C, dk)
    b_last = bc3[:, C - 1:C, :]
    k3 = k_ref[...].reshape(nchunk, C, dk)
    v3 = v_ref[...].astype(BF16).reshape(nchunk, C, dv)
    q_dec = (q_ref[...].reshape(nchunk, C, dk) * scale * jnp.exp(bc3)).astype(BF16)
    k_dec = (k3 * jnp.exp(-bc3)).astype(BF16)
    k_end = (k3 * jnp.exp(b_last - bc3)).astype(BF16)
    causal = lax.broadcasted_iota(jnp.int32, (C, C), 0) >= lax.broadcasted_iota(jnp.int32, (C, C), 1)
    a = jnp.where(causal, jnp.einsum("nid,njd->nij", q_dec, k_dec, preferred_element_type=F32), 0.0)
    o = jnp.einsum("nij,nje->nie", a.astype(BF16), v3, preferred_element_type=F32)
    decay_rows = jnp.concatenate([jnp.exp(b_last).reshape(nchunk, dk), jnp.ones((LANES - nchunk, dk), F32)], axis=0)
    decay_cols = jnp.transpose(decay_rows)
    s = s_ref[0, 0]
    s_starts = []
    for j in range(nchunk):
        s_starts.append(s.astype(BF16))
        s = s * decay_cols[:, j:j + 1] + _dot_tn(k_end[j], v3[j])
    s_ref[0, 0] = s
    o = o + jnp.einsum("nid,nde->nie", q_dec, jnp.stack(s_starts), preferred_element_type=F32)
    o = o.reshape(cb, dv)
    on = o * lax.rsqrt(jnp.mean(o * o, axis=-1, keepdims=True) + LN_EPS) * ng_ref[...]
    rr = r_ref[...]
    o_ref[...] = (on * (rr * jax.nn.sigmoid(rr))).astype(o_ref.dtype)


def _side_cast_specs(sides, nsteps, step_of):
    specs, shapes = [], []
    for s in sides:
        R, C = s.shape
        cr = _side_chunk_rows(R, nsteps)
        index = functools.partial(lambda *g, n: (jnp.minimum(step_of(*g), n - 1), 0), n=R // cr)
        specs.append(pl.BlockSpec((cr, C), index))
        shapes.append(jax.ShapeDtypeStruct((R, C), BF16))
    return specs, shapes


def _gla_prompt(proj, w_a2p, b_a, norm_g, sides=(), *, B, T, H, dk, dv, cb=512):
    N = B * T
    cb = min(cb, T)
    assert T % cb == 0 and cb % GLA_CHUNK == 0
    nc = T // cb
    dkt = H * dk
    dvt = H * dv
    assert (2 * dkt) % dv == 0 and (2 * dkt + 2 * dvt) % LANES == 0
    v0 = (2 * dkt) // dv
    r0 = (2 * dkt + dvt) // dv
    a0 = (2 * dkt + 2 * dvt) // LANES
    rowblk = lambda b, h, c: b * nc + c
    side_specs, side_shapes = _side_cast_specs(sides, B * H * nc, lambda b, h, c: (b * H + h) * nc + c)
    return pl.pallas_call(
        functools.partial(_gla_prompt_kernel, nchunk=cb // GLA_CHUNK, scale=dk ** -0.5, nside=len(sides)),
        grid=(B, H, nc),
        in_specs=[
            pl.BlockSpec((cb, dk), lambda b, h, c: (rowblk(b, h, c), h)),
            pl.BlockSpec((cb, dk), lambda b, h, c: (rowblk(b, h, c), H + h)),
            pl.BlockSpec((cb, dv), lambda b, h, c: (rowblk(b, h, c), v0 + h)),
            pl.BlockSpec((cb, dv), lambda b, h, c: (rowblk(b, h, c), r0 + h)),
            pl.BlockSpec((cb, LANES), lambda b, h, c: (rowblk(b, h, c), a0)),
            pl.BlockSpec((LANES, dk), lambda b, h, c: (0, h)),
            pl.BlockSpec((1, dk), lambda b, h, c: (0, h)),
            pl.BlockSpec((1, dv), lambda b, h, c: (0, 0)),
            *side_specs,
        ],
        out_specs=[
            pl.BlockSpec((cb, dv), lambda b, h, c: (rowblk(b, h, c), h)),
            pl.BlockSpec((1, 1, dk, dv), lambda b, h, c: (b, h, 0, 0)),
            *side_specs,
        ],
        out_shape=[jax.ShapeDtypeStruct((N, dvt), BF16), jax.ShapeDtypeStruct((B, H, dk, dv), F32), *side_shapes],
        compiler_params=_params(*(("arbitrary",) * 3 if sides else ("parallel", "parallel", "arbitrary"))),
        name="gla_prompt",
    )(proj, proj, proj, proj, proj, w_a2p, b_a.reshape(1, dkt), norm_g.reshape(1, dv), *sides)


def _gla_sample_kernel(q_ref, k_ref, v_ref, r_ref, alr_ref, wat_ref, ba_ref, ng_ref, s0_ref, o_ref, s_ref,
                       *, scale, H, dk, dv):
    def column(row):
        return jnp.transpose(jnp.broadcast_to(row, (8, dk)))[:, 0:1]

    for h in range(H):
        rows = slice(h * dk, (h + 1) * dk)
        cols = slice(h * dv, (h + 1) * dv)
        z = jnp.sum(wat_ref[rows, :] * alr_ref[0], axis=-1, keepdims=True) + ba_ref[rows, :]
        decay = jnp.exp(_log_sigmoid(z) / GLA_TAU)
        s_new = s0_ref[0, h] * decay + column(k_ref[0, :, rows]) * v_ref[0, :, cols]
        s_ref[0, h] = s_new
        o = jnp.sum((column(q_ref[0, :, rows]) * scale) * s_new, axis=0, keepdims=True)
        on = o * lax.rsqrt(jnp.mean(o * o, axis=-1, keepdims=True) + LN_EPS) * ng_ref[...]
        rr = r_ref[0, :, cols]
        o_ref[0, :, cols] = (on * (rr * jax.nn.sigmoid(rr))).astype(o_ref.dtype)


def _gla_sample(qkvr, alr, w_a2tp, b_a, norm_g, s0, *, H, dk, dv):
    Bd = qkvr.shape[0]
    dkt = H * dk
    dvt = H * dv
    q = qkvr[:, :dkt].reshape(Bd, 1, dkt)
    k = qkvr[:, dkt:2 * dkt].reshape(Bd, 1, dkt)
    v = qkvr[:, 2 * dkt:2 * dkt + dvt].reshape(Bd, 1, dvt)
    r = qkvr[:, 2 * dkt + dvt:].reshape(Bd, 1, dvt)
    col = lambda b: (b, 0, 0, 0)
    vec = lambda b: (b, 0, 0)
    fixed = lambda b: (0, 0)
    return pl.pallas_call(
        functools.partial(_gla_sample_kernel, scale=dk ** -0.5, H=H, dk=dk, dv=dv),
        grid=(Bd,),
        in_specs=[
            pl.BlockSpec((1, 1, dkt), vec),
            pl.BlockSpec((1, 1, dkt), vec),
            pl.BlockSpec((1, 1, dvt), vec),
            pl.BlockSpec((1, 1, dvt), vec),
            pl.BlockSpec((1, 1, LANES), vec),
            pl.BlockSpec((dkt, LANES), fixed),
            pl.BlockSpec((dkt, 1), fixed),
            pl.BlockSpec((1, dv), fixed),
            pl.BlockSpec((1, H, dk, dv), col),
        ],
        out_specs=[
            pl.BlockSpec((1, 1, dvt), vec),
            pl.BlockSpec((1, H, dk, dv), col),
        ],
        out_shape=[jax.ShapeDtypeStruct((Bd, 1, dvt), BF16), jax.ShapeDtypeStruct((Bd, H, dk, dv), F32)],
        compiler_params=_params("parallel"),
        name="gla_sample",
    )(q, k, v, r, alr.reshape(Bd, 1, LANES), w_a2tp, b_a.reshape(dkt, 1), norm_g.reshape(1, dv), s0)


def _attn_prompt_kernel(q0_ref, q1_ref, q2_ref, k_ref, v_ref, bvec_ref, side_ref, o_ref, side_out_ref,
                        qb, kb, vb, m_s, l_s, acc_s, *, T, dils):
    side_out_ref[...] = side_ref[...].astype(BF16)
    BQ = ATTN_BLOCK
    q_refs = (q0_ref, q1_ref, q2_ref)
    hd = k_ref.shape[2]
    nblk = T // BQ

    def rows_of(start, n, d):
        return pl.ds(start, n) if d == 1 else pl.ds(start, n, stride=d)

    def softmax_pv(s, v):
        m = jnp.max(s, axis=-1, keepdims=True)
        p = jnp.exp(s - m)
        l = jnp.sum(p, axis=-1, keepdims=True)
        pv = jnp.einsum("bqk,bkd->bqd", p.astype(BF16), v, preferred_element_type=F32)
        return m, l, pv

    for g, d in enumerate(dils):
        nb = T // (d * BQ)
        bias = pltpu.roll(jnp.broadcast_to(bvec_ref[g, 0], (BQ, 3 * BQ)), 0, 1, stride=1, stride_axis=0)
        blocks = [(r, 0) for r in range(d)] + [(r, i) for r in range(d) for i in range(1, nb)]
        nfirst = d
        for e, (r, i) in enumerate(blocks):
            qb[e] = q_refs[g][0, rows_of(r + i * BQ * d, BQ, d), :].astype(BF16)
            if i == 0:
                kb[e, 0:BQ, :] = k_ref[0, rows_of(r, BQ, d), :].astype(BF16)
                vb[e, 0:BQ, :] = v_ref[0, rows_of(r, BQ, d), :].astype(BF16)
            else:
                kb[e] = k_ref[0, rows_of(r + (i - 1) * BQ * d, 2 * BQ, d), :].astype(BF16)
                vb[e] = v_ref[0, rows_of(r + (i - 1) * BQ * d, 2 * BQ, d), :].astype(BF16)
        s_first = jnp.einsum("bqd,bkd->bqk", qb[0:nfirst], kb[0:nfirst, 0:BQ, :],
                             preferred_element_type=F32) + bias[:, BQ:2 * BQ]
        stats = [softmax_pv(s_first, vb[0:nfirst, 0:BQ, :])]
        if nb > 1:
            s_rest = jnp.einsum("bqd,bkd->bqk", qb[nfirst:nblk], kb[nfirst:nblk],
                                preferred_element_type=F32) + bias[:, 0:2 * BQ]
            stats.append(softmax_pv(s_rest, vb[nfirst:nblk]))
        for e, (r, i) in enumerate(blocks):
            m, l, pv = stats[0] if e < nfirst else stats[1]
            idx = e if e < nfirst else e - nfirst
            rows = rows_of(r + i * BQ * d, BQ, d)
            m_s[g, rows, :] = jnp.broadcast_to(m[idx], (BQ, hd))
            l_s[g, rows, :] = jnp.broadcast_to(l[idx], (BQ, hd))
            acc_s[g, rows, :] = pv[idx]

    CH = 2 * BQ
    G = len(dils)
    for c in range(T // CH):
        sl = slice(c * CH, (c + 1) * CH)
        ms = [m_s[g, sl, :] for g in range(G)]
        mm = functools.reduce(jnp.maximum, ms)
        ws = [jnp.exp(m - mm) for m in ms]
        num = functools.reduce(lambda a, b: a + b, [ws[g] * acc_s[g, sl, :] for g in range(G)])
        den = functools.reduce(lambda a, b: a + b, [ws[g] * l_s[g, sl, :] for g in range(G)])
        o_ref[0, sl, :] = (num / den).astype(o_ref.dtype)


def _side_chunk_rows(rows, nsteps):
    return min(c for c in range(16, rows + 1, 16) if rows % c == 0 and rows // c <= nsteps)


def _attn_prompt(q, k, v, bias_vecs, side, *, B, T, H, hd):
    dils = tuple(d for _, d in DSW_GROUPS)
    G = len(dils)
    BQ = ATTN_BLOCK
    for w, d in DSW_GROUPS:
        assert w // d == BQ and T % (d * BQ) == 0
    assert hd == LANES
    nblk = T // BQ
    R, C = side.shape
    cr = _side_chunk_rows(R, B * H)
    nchunks = R // cr
    chunk = lambda b, h: (jnp.minimum(b * H + h, nchunks - 1), 0)
    qspec = lambda g: pl.BlockSpec((1, T, hd), lambda b, h: (b, 0, g * H + h))
    kvspec = pl.BlockSpec((1, T, hd), lambda b, h: (b, 0, h))
    return pl.pallas_call(
        functools.partial(_attn_prompt_kernel, T=T, dils=dils),
        grid=(B, H),
        in_specs=[qspec(0), qspec(1), qspec(2), kvspec, kvspec,
                  pl.BlockSpec((G, 1, 1, 3 * BQ), lambda b, h: (0, h, 0, 0)),
                  pl.BlockSpec((cr, C), chunk)],
        out_specs=[pl.BlockSpec((1, T, hd), lambda b, h: (b, 0, h)), pl.BlockSpec((cr, C), chunk)],
        out_shape=[jax.ShapeDtypeStruct((B, T, H * hd), BF16), jax.ShapeDtypeStruct((R, C), BF16)],
        scratch_shapes=[pltpu.VMEM((nblk, BQ, hd), BF16), pltpu.VMEM((nblk, 2 * BQ, hd), BF16),
                        pltpu.VMEM((nblk, 2 * BQ, hd), BF16),
                        pltpu.VMEM((G, T, hd), F32), pltpu.VMEM((G, T, hd), F32), pltpu.VMEM((G, T, hd), F32)],
        compiler_params=_params("arbitrary", "arbitrary"),
        name="attn_prompt",
    )(q, q, q, k, v, bias_vecs, side)


def _attn_sample_kernel(q_ref, kn_ref, vn_ref, ck0, ck1, ck2, cv0, cv1, cv2, bias_ref, o_ref, *, G):
    cks = (ck0, ck1, ck2)
    cvs = (cv0, cv1, cv2)
    BQ = ATTN_BLOCK
    kn = kn_ref[0]
    vn = vn_ref[0]
    s_cache, s_new = [], []
    for g in range(G):
        qg = q_ref[0, g]
        s_cache.append(jnp.sum(cks[g][0] * qg[None], axis=-1, keepdims=True) + bias_ref[g, 0:BQ])
        s_new.append(jnp.sum(kn * qg, axis=-1, keepdims=True) + bias_ref[g, BQ])
    mx = s_new[0]
    for g in range(G):
        mx = jnp.maximum(mx, jnp.maximum(s_new[g], jnp.max(s_cache[g], axis=0)))
    den = jnp.zeros_like(mx)
    o = jnp.zeros(kn.shape, F32)
    for g in range(G):
        pc = jnp.exp(s_cache[g] - mx)
        pn = jnp.exp(s_new[g] - mx)
        den = den + jnp.sum(pc, axis=0) + pn
        o = o + jnp.sum(pc * cvs[g][0], axis=0) + pn * vn
    o_ref[0] = (o / den).astype(o_ref.dtype)


def _attn_sample(q, k_new, v_new, cache_k, cache_v, bias_rows, *, H, hd):
    Bd, Lc = cache_k.shape[:2]
    G = len(DSW_GROUPS)
    BQ = ATTN_BLOCK
    views_k, views_v, specs = [], [], []
    for w, d in DSW_GROUPS:
        assert w // d == BQ and Lc % (d * BQ) == 0
        views_k.append(cache_k.reshape(Bd, Lc // d, d, H, hd))
        views_v.append(cache_v.reshape(Bd, Lc // d, d, H, hd))
        last = Lc // d // BQ - 1
        specs.append(pl.BlockSpec((1, BQ, None, H, hd),
                                  functools.partial(lambda b, last: (b, last, 0, 0, 0), last=last)))
    one = lambda b: (b, 0, 0)
    return pl.pallas_call(
        functools.partial(_attn_sample_kernel, G=G),
        grid=(Bd,),
        in_specs=[pl.BlockSpec((1, G, H, hd), lambda b: (b, 0, 0, 0)),
                  pl.BlockSpec((1, H, hd), one), pl.BlockSpec((1, H, hd), one),
                  *specs, *specs,
                  pl.BlockSpec(bias_rows.shape, lambda b: (0, 0, 0, 0))],
        out_specs=pl.BlockSpec((1, H, hd), one),
        out_shape=jax.ShapeDtypeStruct((Bd, H, hd), BF16),
        compiler_params=_params("parallel"),
        name="attn_sample",
    )(q.reshape(Bd, G, H, hd), k_new.reshape(Bd, H, hd), v_new.reshape(Bd, H, hd),
      *views_k, *views_v, bias_rows)


def _row_copy(src_ref, dst_ref, sem, src_row, dst_row):
    return pltpu.make_async_copy(src_ref.at[pl.ds(src_row, 1), :], dst_ref.at[pl.ds(dst_row, 1), :], sem)


DMA_UNROLL = 8


def _combine_ln_kernel(p0_ref, p1_ref, rows_ref, w_ref, h_ref, g_ref, b_ref, o_ref, buf, sem, *, R, alpha, nsteps):
    i = pl.program_id(0)
    slot = i % 2
    unroll = math.gcd(DMA_UNROLL, R)

    def fetch(step, sl):
        def start(r, c):
            _row_copy(rows_ref, buf.at[sl, 0], sem.at[sl], p0_ref[step * R + r], r).start()
            _row_copy(rows_ref, buf.at[sl, 1], sem.at[sl], p1_ref[step * R + r], r).start()
            return c

        lax.fori_loop(0, R, start, 0, unroll=unroll)

    @pl.when(i == 0)
    def _():
        fetch(0, 0)

    @pl.when(i + 1 < nsteps)
    def _():
        fetch(i + 1, 1 - slot)

    def wait(r, c):
        _row_copy(rows_ref, buf.at[slot, 0], sem.at[slot], 0, r).wait()
        _row_copy(rows_ref, buf.at[slot, 1], sem.at[slot], 0, r).wait()
        return c

    lax.fori_loop(0, R, wait, 0, unroll=unroll)
    w = w_ref[...]
    moe = w[:, 0:1] * buf[slot, 0] + w[:, 1:2] * buf[slot, 1]
    o_ref[...] = _layer_norm(alpha * h_ref[...] + moe, g_ref[...], b_ref[...])


def _combine_ln(rows, p0, p1, w, h, g, b, alpha, R=256):
    M = p0.shape[0]
    D = h.shape[1]
    R = min(R, M)
    assert M % R == 0 and h.shape[0] >= M
    row = lambda i, p0, p1: (i, 0)
    fixed = lambda i, p0, p1: (0, 0)
    return pl.pallas_call(
        functools.partial(_combine_ln_kernel, R=R, alpha=alpha, nsteps=M // R),
        grid_spec=pltpu.PrefetchScalarGridSpec(
            num_scalar_prefetch=2, grid=(M // R,),
            in_specs=[pl.BlockSpec(memory_space=pl.ANY), pl.BlockSpec((R, TOP_K), row), pl.BlockSpec((R, D), row),
                      pl.BlockSpec((1, D), fixed), pl.BlockSpec((1, D), fixed)],
            out_specs=pl.BlockSpec((R, D), row),
            scratch_shapes=[pltpu.VMEM((2, TOP_K, R, D), F32), pltpu.SemaphoreType.DMA((2,))]),
        out_shape=jax.ShapeDtypeStruct((M, D), F32),
        compiler_params=_params("arbitrary"),
        name="moe_combine_ln",
    )(p0, p1, rows, w, h, g.reshape(1, D), b.reshape(1, D))


def _routing_tables(top_i, n_experts, tm):
    nt = top_i.shape[0]
    na = TOP_K * nt
    e = top_i.reshape(na)
    onehot = (e[None, :] == jnp.arange(n_experts, dtype=jnp.int32)[:, None]).astype(jnp.int32)
    csum = jnp.cumsum(onehot, axis=1)
    counts = csum[:, -1]
    padded = ((counts + tm - 1) // tm) * tm
    pend = jnp.cumsum(padded)
    pstart = pend - padded
    slot = jnp.sum(onehot * (csum - 1 + pstart[:, None]), axis=0)
    ntiles = (na + n_experts * (tm - 1) + tm - 1) // tm
    src = jnp.zeros(((ntiles + 1) * tm,), jnp.int32).at[slot].set(
        jnp.arange(na, dtype=jnp.int32) // TOP_K, unique_indices=True, indices_are_sorted=False)
    tile_start = jnp.arange(ntiles, dtype=jnp.int32) * tm
    used = (tile_start < pend[-1]).astype(jnp.int32)
    te = jnp.minimum(jnp.sum((tile_start[:, None] >= pend[None, :]).astype(jnp.int32), axis=1), n_experts - 1)
    last_used = jnp.max(jnp.where(used > 0, te, 0))
    te = jnp.where(used > 0, te, last_used)
    slot2 = slot.reshape(nt, TOP_K)
    return src, te, used, slot2[:, 0], slot2[:, 1]


def _t5_bucket_ids(dist, max_dist):
    max_exact = N_BUCKETS // 2
    d = np.asarray(dist, dtype=np.int64)
    log_ratio = np.log(np.maximum(d, max_exact) / max_exact) / math.log(max_dist / max_exact)
    large = np.minimum(max_exact + (log_ratio * (N_BUCKETS - max_exact)).astype(np.int64), N_BUCKETS - 1)
    return np.where(d < max_exact, d, large).astype(np.int32)


def _group_biases(rel_bias, H):
    max_dist = max(w for w, _ in DSW_GROUPS)
    out = []
    for g, (w, d) in enumerate(DSW_GROUPS):
        buckets = _t5_bucket_ids(d * np.arange(w // d + 1), max_dist)
        out.append(rel_bias[buckets][:, g * H:(g + 1) * H].astype(F32))
    return out


def _prompt_bias_vecs(biases):
    BQ = ATTN_BLOCK
    vecs = []
    for bg in biases:
        v = jnp.concatenate([bg[::-1], jnp.full((2 * BQ - 1, bg.shape[1]), NEG_INF, F32)], axis=0)
        vecs.append(jnp.transpose(v)[:, None, :])
    return jnp.stack(vecs)


def _sample_bias_rows(biases):
    BQ = ATTN_BLOCK
    rows = []
    for bg in biases:
        rows.append(jnp.concatenate([bg[::-1][:BQ], bg[0:1], jnp.zeros((7, bg.shape[1]), F32)], axis=0))
    return jnp.stack(rows)[..., None]


def kernel(x_prompt, x_sample, state_gla, cache_k, cache_v, ln_g, ln_b, gla_w_in, gla_w_a2, gla_b_a,
           gla_norm_g, gla_w_out, kv_w, dsw_w_q, dsw_w_out, rel_bias, ffn_w_gu, ffn_w_down,
           moe_w_router, moe_w_gu, moe_w_down):
    B, T, D = x_prompt.shape
    Bd = x_sample.shape[0]
    depth = ln_g.shape[0]
    assert depth == 2 and x_sample.shape[1] == 1
    alpha = (2 * depth) ** 0.25
    rank, dkt = gla_w_a2.shape[1:]
    dv = gla_norm_g.shape[1]
    dvt = gla_w_out.shape[1]
    Hg = dvt // dv
    dk = dkt // Hg
    G = len(DSW_GROUPS)
    Ha = rel_bias.shape[1] // G
    hd = D // Ha
    n_experts = moe_w_router.shape[2]
    f_dense = ffn_w_down.shape[1]
    f_exp = moe_w_down.shape[2]
    N = B * T

    w_in = gla_w_in[0]
    c_a = 2 * dkt + dvt
    w_in_all = jnp.concatenate([w_in[:, :c_a], w_in[:, c_a + rank:],
                                jnp.pad(w_in[:, c_a:c_a + rank], ((0, 0), (0, LANES - rank)))], axis=1).astype(BF16)
    n_in = w_in_all.shape[1]
    tn_in = max(t for t in range(LANES, 1024 + 1, LANES) if n_in % t == 0)
    w_a2p = jnp.pad(gla_w_a2[0], ((0, LANES - rank), (0, 0)))
    w_a2tp = jnp.pad(gla_w_a2[0].T, ((0, 0), (0, LANES - rank)))
    w_gout = gla_w_out[0].astype(BF16)
    w_k = kv_w[:, :D].astype(BF16)
    w_v = kv_w[:, D:].astype(BF16)
    w_q = dsw_w_q[0].astype(BF16)
    w_aout = dsw_w_out[0].astype(BF16)
    w_router = jnp.pad(moe_w_router[0], ((0, 0), (0, LANES - n_experts)))
    w_router_hi = w_router.astype(BF16)
    w_router_p = jnp.concatenate([w_router_hi, (w_router - w_router_hi.astype(F32)).astype(BF16)], axis=1)
    biases = _group_biases(rel_bias, Ha)
    tf_dense = math.gcd(f_dense, 512)
    tm_moe = 512
    tf_exp = max(t for t in range(256, 1024 + 1, 256) if f_exp % t == 0)

    def layer0_in(x2):
        return _mm(x2, w_in_all, tn=tn_in, name="gla_in_proj")

    def layer0_out(o_gla, x2, side=None):
        m = x2.shape[0]
        h1, h1b = _mm_res_ln(o_gla, w_gout, x2, ln_g[0, 0], ln_b[0, 0], alpha, name="gla_out_ln")
        h2, h2b, *side_out = _ffn(h1b, w_ffn_gu, w_ffn_down, h1, ln_g[0, 1], ln_b[0, 1], alpha,
                                  tm=min(512, m), tf=tf_dense, side=side, name="ffn_dense")
        k = _mm(h2b, w_k, name="k_proj")
        v = _mm(h2b, w_v, name="v_proj")
        q = _mm(h2b, w_q, scale=hd ** -0.5, name="q_proj")
        return (h2, k, v, q, *side_out)

    def attn_out(o_att, h2, **kw):
        return _mm_res_ln(o_att, w_aout, h2, ln_g[1, 0], ln_b[1, 0], alpha, want_bf16=False,
                          router_w=w_router_p, n_experts=n_experts, name="attn_out_ln", **kw)

    xp = x_prompt.reshape(N, D)
    proj_p = layer0_in(xp)
    o_gla_p, state_p, w_ffn_gu, w_ffn_down = _gla_prompt(
        proj_p, w_a2p, gla_b_a[0], gla_norm_g[0], sides=(ffn_w_gu[0], ffn_w_down[0]), B=B, T=T, H=Hg, dk=dk, dv=dv)
    h2_p, k_p, v_p, q_p, w_moe_down = layer0_out(o_gla_p, xp, side=moe_w_down[0].reshape(n_experts * f_exp, D))
    w_moe_down = w_moe_down.reshape(n_experts, f_exp, D)
    o_att_p, w_moe_gu = _attn_prompt(q_p.reshape(B, T, G * D), k_p.reshape(B, T, D), v_p.reshape(B, T, D),
                                     _prompt_bias_vecs(biases), moe_w_gu[0].reshape(n_experts * D, 2 * f_exp),
                                     B=B, T=T, H=Ha, hd=hd)
    w_moe_gu = w_moe_gu.reshape(n_experts, D, 2 * f_exp)
    h3, route_p = attn_out(o_att_p.reshape(N, D), h2_p, extra_rows=Bd)

    xs = x_sample.reshape(Bd, D)
    proj_s = layer0_in(xs)
    o_gla_s, state_s = _gla_sample(proj_s[:, :n_in - LANES], proj_s[:, n_in - LANES:], w_a2tp, gla_b_a[0],
                                   gla_norm_g[0], state_gla[0], H=Hg, dk=dk, dv=dv)
    h2_s, k_s, v_s, q_s = layer0_out(o_gla_s.reshape(Bd, dvt), xs)
    o_att_s = _attn_sample(q_s, k_s, v_s, cache_k, cache_v, _sample_bias_rows(biases), H=Ha, hd=hd)
    h3, route_s = attn_out(o_att_s.reshape(Bd, D), h2_s, into=h3)

    route = jnp.concatenate([route_p, route_s], axis=0)
    top_i = route[:, 0:TOP_K].astype(jnp.int32)
    top_w = route[:, TOP_K:2 * TOP_K]
    src, te, used, p0, p1 = _routing_tables(top_i, n_experts, tm_moe)
    y_sorted = _moe_ffn(h3, src, w_moe_gu, w_moe_down, te, used, tm=tm_moe, tf=tf_exp)
    y_p = _combine_ln(y_sorted, p0[:N], p1[:N], top_w[:N], h3, ln_g[1, 1], ln_b[1, 1], alpha)
    y_s = _combine_ln(y_sorted, p0[N:], p1[N:], top_w[N:], h3[N:], ln_g[1, 1], ln_b[1, 1], alpha)

    return (y_p.reshape(B, T, D), y_s.reshape(Bd, 1, D),
            state_p[None], state_s[None],
            k_p.reshape(B, T, Ha, hd), v_p.reshape(B, T, Ha, hd),
            k_s.reshape(Bd, 1, Ha, hd), v_s.reshape(Bd, 1, Ha, hd))
```

```python
import functools
import math

import jax
import jax.numpy as jnp
import numpy as np
from jax import lax
from jax.experimental import pallas as pl
from jax.experimental.pallas import tpu as pltpu

F32 = jnp.float32
BF16 = jnp.bfloat16

GLA_TAU = 16.0
GLA_CHUNK = 64
LN_EPS = 1e-5
NEG_INF = -1e30
DSW_GROUPS = ((128, 1), (512, 4), (2048, 16))
N_BUCKETS = 32
TOP_K = 2

LANES = 128
VMEM_LIMIT_BYTES = 56 * 1024 * 1024
ATTN_BLOCK = 128


def _params(*sem):
    return pltpu.CompilerParams(dimension_semantics=sem, vmem_limit_bytes=VMEM_LIMIT_BYTES)


def _layer_norm(x, g, b):
    mu = jnp.mean(x, axis=-1, keepdims=True)
    xc = x - mu
    var = jnp.mean(xc * xc, axis=-1, keepdims=True)
    return xc * lax.rsqrt(var + LN_EPS) * g + b


def _dot(a, b):
    return jnp.dot(a, b, preferred_element_type=F32)


def _dot_nt(a, b):
    return lax.dot_general(a, b, (((1,), (1,)), ((), ())), preferred_element_type=F32)


def _dot_tn(a, b):
    return lax.dot_general(a, b, (((0,), (0,)), ((), ())), preferred_element_type=F32)


def _dot_hi_lo(a, b):
    a_hi = a.astype(BF16)
    b_hi = b.astype(BF16)
    a_lo = (a - a_hi.astype(F32)).astype(BF16)
    b_lo = (b - b_hi.astype(F32)).astype(BF16)
    return _dot(a_hi, b_hi) + (_dot(a_lo, b_hi) + _dot(a_hi, b_lo))


def _mm_kernel(a_ref, w_ref, o_ref, *, scale):
    acc = _dot(a_ref[...].astype(BF16), w_ref[...])
    if scale != 1.0:
        acc = acc * scale
    o_ref[...] = acc


def _mm(a, w, scale=1.0, tm=1024, tn=1024, name="mm"):
    M, K = a.shape
    N = w.shape[1]
    tm = min(tm, M)
    tn = math.gcd(tn, N)
    assert M % tm == 0 and N % tn == 0 and tn % LANES == 0, (M, N, tm, tn)
    return pl.pallas_call(
        functools.partial(_mm_kernel, scale=scale),
        grid=(M // tm, N // tn),
        in_specs=[pl.BlockSpec((tm, K), lambda i, j: (i, 0)),
                  pl.BlockSpec((K, tn), lambda i, j: (0, j))],
        out_specs=pl.BlockSpec((tm, tn), lambda i, j: (i, j)),
        out_shape=jax.ShapeDtypeStruct((M, N), F32),
        compiler_params=_params("parallel", "arbitrary"),
        name=name,
    )(a, w)


def _top2_route(logits, n_experts):
    lane = lax.broadcasted_iota(jnp.int32, logits.shape, 1)
    lanef = lane.astype(F32)
    logits = jnp.where(lane < n_experts, logits, -jnp.inf)
    v1 = jnp.max(logits, axis=-1, keepdims=True)
    i1 = jnp.min(jnp.where(logits == v1, lanef, float(LANES)), axis=-1, keepdims=True)
    rest = jnp.where(lanef == i1, -jnp.inf, logits)
    v2 = jnp.max(rest, axis=-1, keepdims=True)
    i2 = jnp.min(jnp.where(rest == v2, lanef, float(LANES)), axis=-1, keepdims=True)
    e2 = jnp.exp(v2 - v1)
    w1 = 1.0 / (1.0 + e2)
    w2 = e2 / (1.0 + e2)
    return jnp.where(lane == 0, i1, jnp.where(lane == 1, i2, jnp.where(lane == 2, w1, w2)))


def _mm_res_ln_kernel(*refs, alpha, want_bf16, n_experts, aliased):
    refs = list(refs)
    a_ref, w_ref, res_ref, g_ref, b_ref = refs[:5]
    del refs[:5]
    if n_experts:
        wr_ref = refs.pop(0)
    if aliased:
        refs.pop(0)
    o_ref = refs.pop(0)
    mix = _dot(a_ref[...], w_ref[...])
    y = _layer_norm(alpha * res_ref[...] + mix, g_ref[...], b_ref[...])
    o_ref[...] = y
    if want_bf16:
        refs.pop(0)[...] = y.astype(BF16)
    if n_experts:
        y_hi = y.astype(BF16)
        y_lo = (y - y_hi.astype(F32)).astype(BF16)
        part = _dot(y_hi, wr_ref[...])
        logits = part[:, 0:LANES] + part[:, LANES:2 * LANES] + _dot(y_lo, wr_ref[:, 0:LANES])
        refs.pop(0)[...] = _top2_route(logits, n_experts)


def _mm_res_ln(a, w, res, g, b, alpha, *, tm=512, want_bf16=True, router_w=None, n_experts=0,
               extra_rows=0, into=None, name="mm_res_ln"):
    M, K = a.shape
    N = w.shape[1]
    tm = min(tm, M)
    assert M % tm == 0 and extra_rows <= tm
    nsteps = M // tm + (1 if extra_rows else 0)
    row = lambda i: (jnp.minimum(i, M // tm - 1), 0)
    fixed = lambda i: (0, 0)
    in_specs = [pl.BlockSpec((tm, K), row), pl.BlockSpec((K, N), fixed),
                pl.BlockSpec((tm, N), row), pl.BlockSpec((1, N), fixed), pl.BlockSpec((1, N), fixed)]
    args = [a, w, res, g.reshape(1, N), b.reshape(1, N)]
    if n_experts:
        in_specs.append(pl.BlockSpec((N, 2 * LANES), fixed))
        args.append(router_w)
    aliases = {}
    rows, off = M + extra_rows, 0
    if into is not None:
        rows = into.shape[0]
        assert (rows - M) % tm == 0
        off = (rows - M) // tm
        aliases = {len(args): 0}
        in_specs.append(pl.BlockSpec(memory_space=pl.ANY))
        args.append(into)
    out_specs = [pl.BlockSpec((tm, N), lambda i: (i + off, 0))]
    out_shape = [jax.ShapeDtypeStruct((rows, N), F32)]
    if want_bf16:
        out_specs.append(pl.BlockSpec((tm, N), row))
        out_shape.append(jax.ShapeDtypeStruct((M, N), BF16))
    if n_experts:
        out_specs.append(pl.BlockSpec((tm, LANES), row))
        out_shape.append(jax.ShapeDtypeStruct((M, LANES), F32))
    outs = pl.pallas_call(
        functools.partial(_mm_res_ln_kernel, alpha=alpha, want_bf16=want_bf16, n_experts=n_experts,
                          aliased=into is not None),
        grid=(nsteps,),
        in_specs=in_specs, out_specs=out_specs, out_shape=out_shape,
        input_output_aliases=aliases,
        compiler_params=_params("arbitrary" if extra_rows else "parallel"),
        name=name,
    )(*args)
    return outs[0] if len(outs) == 1 else outs


def _ffn_kernel(x_ref, wg_ref, wu_ref, wd_ref, res_ref, g_ref, b_ref, *rest, alpha, nf, has_side):
    if has_side:
        side_ref, o_ref, obf_ref, side_out_ref = rest
        side_out_ref[...] = side_ref[...].astype(BF16)
    else:
        o_ref, obf_ref = rest
    f = pl.program_id(1)

    @pl.when(f == 0)
    def _():
        o_ref[...] = jnp.zeros_like(o_ref)

    x = x_ref[...]
    g = _dot(x, wg_ref[...])
    u = _dot(x, wu_ref[...])
    act = (g * jax.nn.sigmoid(g) * u).astype(BF16)
    o_ref[...] += _dot(act, wd_ref[...])

    @pl.when(f == nf - 1)
    def _():
        y = _layer_norm(alpha * res_ref[...] + o_ref[...], g_ref[...], b_ref[...])
        o_ref[...] = y
        obf_ref[...] = y.astype(BF16)


def _ffn(x, w_gu, w_down, res, g, b, alpha, *, tm, tf, side=None, name="ffn"):
    M, D = x.shape
    F = w_down.shape[0]
    assert M % tm == 0 and F % tf == 0
    nf = F // tf
    nsteps = (M // tm) * nf
    row = lambda i, f: (i, 0)
    fixed = lambda i, f: (0, 0)
    in_specs = [
        pl.BlockSpec((tm, D), row),
        pl.BlockSpec((D, tf), lambda i, f: (0, f)),
        pl.BlockSpec((D, tf), lambda i, f: (0, nf + f)),
        pl.BlockSpec((tf, D), lambda i, f: (f, 0)),
        pl.BlockSpec((tm, D), row), pl.BlockSpec((1, D), fixed), pl.BlockSpec((1, D), fixed),
    ]
    args = [x, w_gu, w_gu, w_down, res, g.reshape(1, D), b.reshape(1, D)]
    out_specs = [pl.BlockSpec((tm, D), row), pl.BlockSpec((tm, D), row)]
    out_shape = [jax.ShapeDtypeStruct((M, D), F32), jax.ShapeDtypeStruct((M, D), BF16)]
    if side is not None:
        R, C = side.shape
        cr = _side_chunk_rows(R, nsteps)
        nchunks = R // cr
        chunk = lambda i, f: (jnp.minimum(i * nf + f, nchunks - 1), 0)
        in_specs.append(pl.BlockSpec((cr, C), chunk))
        args.append(side)
        out_specs.append(pl.BlockSpec((cr, C), chunk))
        out_shape.append(jax.ShapeDtypeStruct((R, C), BF16))
    return pl.pallas_call(
        functools.partial(_ffn_kernel, alpha=alpha, nf=nf, has_side=side is not None),
        grid=(M // tm, nf), in_specs=in_specs, out_specs=out_specs, out_shape=out_shape,
        compiler_params=_params("arbitrary" if side is not None else "parallel", "arbitrary"),
        name=name,
    )(*args)


def _moe_ffn_kernel(te_ref, used_ref, src_ref, h_ref, wg_ref, wu_ref, wd_ref, o_ref, xbuf, sem, *, nf, tm):
    i = pl.program_id(0)
    f = pl.program_id(1)
    share = -(-tm // nf)
    slot = i % 2

    def row_copy(src_slot, buf, row):
        return pltpu.make_async_copy(h_ref.at[pl.ds(src_ref[src_slot], 1), :],
                                     xbuf.at[buf, pl.ds(row, 1), :], sem)

    @pl.when(jnp.logical_and(i == 0, f == 0))
    def _():
        def start(r, c):
            row_copy(r, 0, r).start()
            return c

        def wait(r, c):
            row_copy(0, 0, r).wait()
            return c

        lax.fori_loop(0, tm, start, 0, unroll=math.gcd(DMA_UNROLL, tm))
        lax.fori_loop(0, tm, wait, 0, unroll=math.gcd(DMA_UNROLL, tm))

    @pl.when(f == 0)
    def _():
        o_ref[...] = jnp.zeros_like(o_ref)

    @pl.when(used_ref[i] != 0)
    def _():
        copies = [row_copy((i + 1) * (share * nf) + f * share + r, 1 - slot, f * share + r) for r in range(share)]
        for cp in copies:
            cp.start()
        x = xbuf[slot, 0:tm, :].astype(BF16)
        g = _dot(x, wg_ref[0])
        u = _dot(x, wu_ref[0])
        act = (g * jax.nn.sigmoid(g) * u).astype(BF16)
        o_ref[...] += _dot(act, wd_ref[0])
        for cp in copies:
            cp.wait()


def _moe_ffn(h, src, w_gu, w_down, tile_expert, tile_used, *, tm, tf, name="ffn_moe"):
    D = h.shape[1]
    E, F, _ = w_down.shape
    ntiles = tile_expert.shape[0]
    assert src.shape[0] == (ntiles + 1) * tm and F % tf == 0
    nf = F // tf
    stride = -(-tm // nf) * nf
    src = jnp.pad(src.reshape(ntiles + 1, tm), ((0, 0), (0, stride - tm))).reshape(-1)
    buf_rows = -(-stride // 8) * 8
    return pl.pallas_call(
        functools.partial(_moe_ffn_kernel, nf=nf, tm=tm),
        grid_spec=pltpu.PrefetchScalarGridSpec(
            num_scalar_prefetch=3, grid=(ntiles, nf),
            in_specs=[
                pl.BlockSpec(memory_space=pl.ANY),
                pl.BlockSpec((1, D, tf), lambda i, f, te, us, src: (te[i], 0, f * us[i])),
                pl.BlockSpec((1, D, tf), lambda i, f, te, us, src: (te[i], 0, nf + f * us[i])),
                pl.BlockSpec((1, tf, D), lambda i, f, te, us, src: (te[i], f * us[i], 0)),
            ],
            out_specs=pl.BlockSpec((tm, D), lambda i, f, te, us, src: (i, 0)),
            scratch_shapes=[pltpu.VMEM((2, buf_rows, D), F32), pltpu.SemaphoreType.DMA(())]),
        out_shape=jax.ShapeDtypeStruct((ntiles * tm, D), F32),
        compiler_params=_params("arbitrary", "arbitrary"),
        name=name,
    )(tile_expert, tile_used, src, h, w_gu, w_gu, w_down)


def _log_sigmoid(z):
    return jnp.minimum(z, 0.0) - jnp.log(1.0 + jnp.exp(-jnp.abs(z)))


def _gla_prompt_kernel(q_ref, k_ref, v_ref, r_ref, alr_ref, wa_ref, ba_ref, ng_ref, *rest, nchunk, scale, nside):
    side_refs = rest[:nside]
    o_ref, s_ref = rest[nside:nside + 2]
    for src, dst in zip(side_refs, rest[nside + 2:]):
        dst[...] = src[...].astype(BF16)
    C = GLA_CHUNK
    dk = q_ref.shape[1]

    @pl.when(pl.program_id(2) == 0)
    def _():
        s_ref[...] = jnp.zeros_like(s_ref)

    cb = nchunk * C
    dv = v_ref.shape[1]
    z = _dot_hi_lo(alr_ref[...], wa_ref[...]) + ba_ref[...]
    bcum = _log_sigmoid(z) / GLA_TAU
    row_in_chunk = lax.broadcasted_iota(jnp.int32, (cb, dk), 0) % C
    shift = 1
    while shift < C:
        bcum = bcum + jnp.where(row_in_chunk >= shift, pltpu.roll(bcum, shift, 0), 0.0)
        shift *= 2
    bc3 = bcum.reshape(nchunk, C, dk)
    b_last = bc3[:, C - 1:C, :]
    k3 = k_ref[...].reshape(nchunk, C, dk)
    v3 = v_ref[...].astype(BF16).reshape(nchunk, C, dv)
    q_dec = (q_ref[...].reshape(nchunk, C, dk) * scale * jnp.exp(bc3)).astype(BF16)
    k_dec = (k3 * jnp.exp(-bc3)).astype(BF16)
    k_end = (k3 * jnp.exp(b_last - bc3)).astype(BF16)
    causal = lax.broadcasted_iota(jnp.int32, (C, C), 0) >= lax.broadcasted_iota(jnp.int32, (C, C), 1)
    a = jnp.where(causal, jnp.einsum("nid,njd->nij", q_dec, k_dec, preferred_element_type=F32), 0.0)
    o = jnp.einsum("nij,nje->nie", a.astype(BF16), v3, preferred_element_type=F32)
    decay_rows = jnp.concatenate([jnp.exp(b_last).reshape(nchunk, dk), jnp.ones((LANES - nchunk, dk), F32)], axis=0)
    decay_cols = jnp.transpose(decay_rows)
    s = s_ref[0, 0]
    s_starts = []
    for j in range(nchunk):
        s_starts.append(s.astype(BF16))
        s = s * decay_cols[:, j:j + 1] + _dot_tn(k_end[j], v3[j])
    s_ref[0, 0] = s
    o = o + jnp.einsum("nid,nde->nie", q_dec, jnp.stack(s_starts), preferred_element_type=F32)
    o = o.reshape(cb, dv)
    on = o * lax.rsqrt(jnp.mean(o * o, axis=-1, keepdims=True) + LN_EPS) * ng_ref[...]
    rr = r_ref[...]
    o_ref[...] = (on * (rr * jax.nn.sigmoid(rr))).astype(o_ref.dtype)


def _side_cast_specs(sides, nsteps, step_of):
    specs, shapes = [], []
    for s in sides:
        R, C = s.shape
        cr = _side_chunk_rows(R, nsteps)
        index = functools.partial(lambda *g, n: (jnp.minimum(step_of(*g), n - 1), 0), n=R // cr)
        specs.append(pl.BlockSpec((cr, C), index))
        shapes.append(jax.ShapeDtypeStruct((R, C), BF16))
    return specs, shapes


def _gla_prompt(proj, w_a2p, b_a, norm_g, sides=(), *, B, T, H, dk, dv, cb=512):
    N = B * T
    cb = min(cb, T)
    assert T % cb == 0 and cb % GLA_CHUNK == 0
    nc = T // cb
    dkt = H * dk
    dvt = H * dv
    assert (2 * dkt) % dv == 0 and (2 * dkt + 2 * dvt) % LANES == 0
    v0 = (2 * dkt) // dv
    r0 = (2 * dkt + dvt) // dv
    a0 = (2 * dkt + 2 * dvt) // LANES
    rowblk = lambda b, h, c: b * nc + c
    side_specs, side_shapes = _side_cast_specs(sides, B * H * nc, lambda b, h, c: (b * H + h) * nc + c)
    return pl.pallas_call(
        functools.partial(_gla_prompt_kernel, nchunk=cb // GLA_CHUNK, scale=dk ** -0.5, nside=len(sides)),
        grid=(B, H, nc),
        in_specs=[
            pl.BlockSpec((cb, dk), lambda b, h, c: (rowblk(b, h, c), h)),
            pl.BlockSpec((cb, dk), lambda b, h, c: (rowblk(b, h, c), H + h)),
            pl.BlockSpec((cb, dv), lambda b, h, c: (rowblk(b, h, c), v0 + h)),
            pl.BlockSpec((cb, dv), lambda b, h, c: (rowblk(b, h, c), r0 + h)),
            pl.BlockSpec((cb, LANES), lambda b, h, c: (rowblk(b, h, c), a0)),
            pl.BlockSpec((LANES, dk), lambda b, h, c: (0, h)),
            pl.BlockSpec((1, dk), lambda b, h, c: (0, h)),
            pl.BlockSpec((1, dv), lambda b, h, c: (0, 0)),
            *side_specs,
        ],
        out_specs=[
            pl.BlockSpec((cb, dv), lambda b, h, c: (rowblk(b, h, c), h)),
            pl.BlockSpec((1, 1, dk, dv), lambda b, h, c: (b, h, 0, 0)),
            *side_specs,
        ],
        out_shape=[jax.ShapeDtypeStruct((N, dvt), BF16), jax.ShapeDtypeStruct((B, H, dk, dv), F32), *side_shapes],
        compiler_params=_params(*(("arbitrary",) * 3 if sides else ("parallel", "parallel", "arbitrary"))),
        name="gla_prompt",
    )(proj, proj, proj, proj, proj, w_a2p, b_a.reshape(1, dkt), norm_g.reshape(1, dv), *sides)


def _gla_sample_kernel(q_ref, k_ref, v_ref, r_ref, alr_ref, wat_ref, ba_ref, ng_ref, s0_ref, o_ref, s_ref,
                       *, scale, H, dk, dv):
    def column(row):
        return jnp.transpose(jnp.broadcast_to(row, (8, dk)))[:, 0:1]

    for h in range(H):
        rows = slice(h * dk, (h + 1) * dk)
        cols = slice(h * dv, (h + 1) * dv)
        z = jnp.sum(wat_ref[rows, :] * alr_ref[0], axis=-1, keepdims=True) + ba_ref[rows, :]
        decay = jnp.exp(_log_sigmoid(z) / GLA_TAU)
        s_new = s0_ref[0, h] * decay + column(k_ref[0, :, rows]) * v_ref[0, :, cols]
        s_ref[0, h] = s_new
        o = jnp.sum((column(q_ref[0, :, rows]) * scale) * s_new, axis=0, keepdims=True)
        on = o * lax.rsqrt(jnp.mean(o * o, axis=-1, keepdims=True) + LN_EPS) * ng_ref[...]
        rr = r_ref[0, :, cols]
        o_ref[0, :, cols] = (on * (rr * jax.nn.sigmoid(rr))).astype(o_ref.dtype)


def _gla_sample(qkvr, alr, w_a2tp, b_a, norm_g, s0, *, H, dk, dv):
    Bd = qkvr.shape[0]
    dkt = H * dk
    dvt = H * dv
    q = qkvr[:, :dkt].reshape(Bd, 1, dkt)
    k = qkvr[:, dkt:2 * dkt].reshape(Bd, 1, dkt)
    v = qkvr[:, 2 * dkt:2 * dkt + dvt].reshape(Bd, 1, dvt)
    r = qkvr[:, 2 * dkt + dvt:].reshape(Bd, 1, dvt)
    col = lambda b: (b, 0, 0, 0)
    vec = lambda b: (b, 0, 0)
    fixed = lambda b: (0, 0)
    return pl.pallas_call(
        functools.partial(_gla_sample_kernel, scale=dk ** -0.5, H=H, dk=dk, dv=dv),
        grid=(Bd,),
        in_specs=[
            pl.BlockSpec((1, 1, dkt), vec),
            pl.BlockSpec((1, 1, dkt), vec),
            pl.BlockSpec((1, 1, dvt), vec),
            pl.BlockSpec((1, 1, dvt), vec),
            pl.BlockSpec((1, 1, LANES), vec),
            pl.BlockSpec((dkt, LANES), fixed),
            pl.BlockSpec((dkt, 1), fixed),
            pl.BlockSpec((1, dv), fixed),
            pl.BlockSpec((1, H, dk, dv), col),
        ],
        out_specs=[
            pl.BlockSpec((1, 1, dvt), vec),
            pl.BlockSpec((1, H, dk, dv), col),
        ],
        out_shape=[jax.ShapeDtypeStruct((Bd, 1, dvt), BF16), jax.ShapeDtypeStruct((Bd, H, dk, dv), F32)],
        compiler_params=_params("parallel"),
        name="gla_sample",
    )(q, k, v, r, alr.reshape(Bd, 1, LANES), w_a2tp, b_a.reshape(dkt, 1), norm_g.reshape(1, dv), s0)


def _attn_prompt_kernel(q0_ref, q1_ref, q2_ref, k_ref, v_ref, bvec_ref, side_ref, o_ref, side_out_ref,
                        qb, kb, vb, m_s, l_s, acc_s, *, T, dils):
    side_out_ref[...] = side_ref[...].astype(BF16)
    BQ = ATTN_BLOCK
    q_refs = (q0_ref, q1_ref, q2_ref)
    hd = k_ref.shape[2]
    nblk = T // BQ

    def rows_of(start, n, d):
        return pl.ds(start, n) if d == 1 else pl.ds(start, n, stride=d)

    def softmax_pv(s, v):
        m = jnp.max(s, axis=-1, keepdims=True)
        p = jnp.exp(s - m)
        l = jnp.sum(p, axis=-1, keepdims=True)
        pv = jnp.einsum("bqk,bkd->bqd", p.astype(BF16), v, preferred_element_type=F32)
        return m, l, pv

    for g, d in enumerate(dils):
        nb = T // (d * BQ)
        bias = pltpu.roll(jnp.broadcast_to(bvec_ref[g, 0], (BQ, 3 * BQ)), 0, 1, stride=1, stride_axis=0)
        blocks = [(r, 0) for r in range(d)] + [(r, i) for r in range(d) for i in range(1, nb)]
        nfirst = d
        for e, (r, i) in enumerate(blocks):
            qb[e] = q_refs[g][0, rows_of(r + i * BQ * d, BQ, d), :].astype(BF16)
            if i == 0:
                kb[e, 0:BQ, :] = k_ref[0, rows_of(r, BQ, d), :].astype(BF16)
                vb[e, 0:BQ, :] = v_ref[0, rows_of(r, BQ, d), :].astype(BF16)
            else:
                kb[e] = k_ref[0, rows_of(r + (i - 1) * BQ * d, 2 * BQ, d), :].astype(BF16)
                vb[e] = v_ref[0, rows_of(r + (i - 1) * BQ * d, 2 * BQ, d), :].astype(BF16)
        s_first = jnp.einsum("bqd,bkd->bqk", qb[0:nfirst], kb[0:nfirst, 0:BQ, :],
                             preferred_element_type=F32) + bias[:, BQ:2 * BQ]
        stats = [softmax_pv(s_first, vb[0:nfirst, 0:BQ, :])]
        if nb > 1:
            s_rest = jnp.einsum("bqd,bkd->bqk", qb[nfirst:nblk], kb[nfirst:nblk],
                                preferred_element_type=F32) + bias[:, 0:2 * BQ]
            stats.append(softmax_pv(s_rest, vb[nfirst:nblk]))
        for e, (r, i) in enumerate(blocks):
            m, l, pv = stats[0] if e < nfirst else stats[1]
            idx = e if e < nfirst else e - nfirst
            rows = rows_of(r + i * BQ * d, BQ, d)
            m_s[g, rows, :] = jnp.broadcast_to(m[idx], (BQ, hd))
            l_s[g, rows, :] = jnp.broadcast_to(l[idx], (BQ, hd))
            acc_s[g, rows, :] = pv[idx]

    CH = 2 * BQ
    G = len(dils)
    for c in range(T // CH):
        sl = slice(c * CH, (c + 1) * CH)
        ms = [m_s[g, sl, :] for g in range(G)]
        mm = functools.reduce(jnp.maximum, ms)
        ws = [jnp.exp(m - mm) for m in ms]
        num = functools.reduce(lambda a, b: a + b, [ws[g] * acc_s[g, sl, :] for g in range(G)])
        den = functools.reduce(lambda a, b: a + b, [ws[g] * l_s[g, sl, :] for g in range(G)])
        o_ref[0, sl, :] = (num / den).astype(o_ref.dtype)


def _side_chunk_rows(rows, nsteps):
    return min(c for c in range(16, rows + 1, 16) if rows % c == 0 and rows // c <= nsteps)


def _attn_prompt(q, k, v, bias_vecs, side, *, B, T, H, hd):
    dils = tuple(d for _, d in DSW_GROUPS)
    G = len(dils)
    BQ = ATTN_BLOCK
    for w, d in DSW_GROUPS:
        assert w // d == BQ and T % (d * BQ) == 0
    assert hd == LANES
    nblk = T // BQ
    R, C = side.shape
    cr = _side_chunk_rows(R, B * H)
    nchunks = R // cr
    chunk = lambda b, h: (jnp.minimum(b * H + h, nchunks - 1), 0)
    qspec = lambda g: pl.BlockSpec((1, T, hd), lambda b, h: (b, 0, g * H + h))
    kvspec = pl.BlockSpec((1, T, hd), lambda b, h: (b, 0, h))
    return pl.pallas_call(
        functools.partial(_attn_prompt_kernel, T=T, dils=dils),
        grid=(B, H),
        in_specs=[qspec(0), qspec(1), qspec(2), kvspec, kvspec,
                  pl.BlockSpec((G, 1, 1, 3 * BQ), lambda b, h: (0, h, 0, 0)),
                  pl.BlockSpec((cr, C), chunk)],
        out_specs=[pl.BlockSpec((1, T, hd), lambda b, h: (b, 0, h)), pl.BlockSpec((cr, C), chunk)],
        out_shape=[jax.ShapeDtypeStruct((B, T, H * hd), BF16), jax.ShapeDtypeStruct((R, C), BF16)],
        scratch_shapes=[pltpu.VMEM((nblk, BQ, hd), BF16), pltpu.VMEM((nblk, 2 * BQ, hd), BF16),
                        pltpu.VMEM((nblk, 2 * BQ, hd), BF16),
                        pltpu.VMEM((G, T, hd), F32), pltpu.VMEM((G, T, hd), F32), pltpu.VMEM((G, T, hd), F32)],
        compiler_params=_params("arbitrary", "arbitrary"),
        name="attn_prompt",
    )(q, q, q, k, v, bias_vecs, side)


def _attn_sample_kernel(q_ref, kn_ref, vn_ref, ck0, ck1, ck2, cv0, cv1, cv2, bias_ref, o_ref, *, G):
    cks = (ck0, ck1, ck2)
    cvs = (cv0, cv1, cv2)
    BQ = ATTN_BLOCK
    kn = kn_ref[0]
    vn = vn_ref[0]
    s_cache, s_new = [], []
    for g in range(G):
        qg = q_ref[0, g]
        s_cache.append(jnp.sum(cks[g][0] * qg[None], axis=-1, keepdims=True) + bias_ref[g, 0:BQ])
        s_new.append(jnp.sum(kn * qg, axis=-1, keepdims=True) + bias_ref[g, BQ])
    mx = s_new[0]
    for g in range(G):
        mx = jnp.maximum(mx, jnp.maximum(s_new[g], jnp.max(s_cache[g], axis=0)))
    den = jnp.zeros_like(mx)
    o = jnp.zeros(kn.shape, F32)
    for g in range(G):
        pc = jnp.exp(s_cache[g] - mx)
        pn = jnp.exp(s_new[g] - mx)
        den = den + jnp.sum(pc, axis=0) + pn
        o = o + jnp.sum(pc * cvs[g][0], axis=0) + pn * vn
    o_ref[0] = (o / den).astype(o_ref.dtype)


def _attn_sample(q, k_new, v_new, cache_k, cache_v, bias_rows, *, H, hd):
    Bd, Lc = cache_k.shape[:2]
    G = len(DSW_GROUPS)
    BQ = ATTN_BLOCK
    views_k, views_v, specs = [], [], []
    for w, d in DSW_GROUPS:
        assert w // d == BQ and Lc % (d * BQ) == 0
        views_k.append(cache_k.reshape(Bd, Lc // d, d, H, hd))
        views_v.append(cache_v.reshape(Bd, Lc // d, d, H, hd))
        last = Lc // d // BQ - 1
        specs.append(pl.BlockSpec((1, BQ, None, H, hd),
                                  functools.partial(lambda b, last: (b, last, 0, 0, 0), last=last)))
    one = lambda b: (b, 0, 0)
    return pl.pallas_call(
        functools.partial(_attn_sample_kernel, G=G),
        grid=(Bd,),
        in_specs=[pl.BlockSpec((1, G, H, hd), lambda b: (b, 0, 0, 0)),
                  pl.BlockSpec((1, H, hd), one), pl.BlockSpec((1, H, hd), one),
                  *specs, *specs,
                  pl.BlockSpec(bias_rows.shape, lambda b: (0, 0, 0, 0))],
        out_specs=pl.BlockSpec((1, H, hd), one),
        out_shape=jax.ShapeDtypeStruct((Bd, H, hd), BF16),
        compiler_params=_params("parallel"),
        name="attn_sample",
    )(q.reshape(Bd, G, H, hd), k_new.reshape(Bd, H, hd), v_new.reshape(Bd, H, hd),
      *views_k, *views_v, bias_rows)


def _row_copy(src_ref, dst_ref, sem, src_row, dst_row):
    return pltpu.make_async_copy(src_ref.at[pl.ds(src_row, 1), :], dst_ref.at[pl.ds(dst_row, 1), :], sem)


DMA_UNROLL = 8


def _combine_ln_kernel(p0_ref, p1_ref, rows_ref, w_ref, h_ref, g_ref, b_ref, o_ref, buf, sem, *, R, alpha, nsteps):
    i = pl.program_id(0)
    slot = i % 2
    unroll = math.gcd(DMA_UNROLL, R)

    def fetch(step, sl):
        def start(r, c):
            _row_copy(rows_ref, buf.at[sl, 0], sem.at[sl], p0_ref[step * R + r], r).start()
            _row_copy(rows_ref, buf.at[sl, 1], sem.at[sl], p1_ref[step * R + r], r).start(priority=1)
            return c

        lax.fori_loop(0, R, start, 0, unroll=unroll)

    @pl.when(i == 0)
    def _():
        fetch(0, 0)

    @pl.when(i + 1 < nsteps)
    def _():
        fetch(i + 1, 1 - slot)

    def wait(r, c):
        _row_copy(rows_ref, buf.at[slot, 0], sem.at[slot], 0, r).wait()
        _row_copy(rows_ref, buf.at[slot, 1], sem.at[slot], 0, r).wait()
        return c

    lax.fori_loop(0, R, wait, 0, unroll=unroll)
    w = w_ref[...]
    moe = w[:, 0:1] * buf[slot, 0] + w[:, 1:2] * buf[slot, 1]
    o_ref[...] = _layer_norm(alpha * h_ref[...] + moe, g_ref[...], b_ref[...])


def _combine_ln(rows, p0, p1, w, h, g, b, alpha, R=256):
    M = p0.shape[0]
    D = h.shape[1]
    R = min(R, M)
    assert M % R == 0 and h.shape[0] >= M
    row = lambda i, p0, p1: (i, 0)
    fixed = lambda i, p0, p1: (0, 0)
    return pl.pallas_call(
        functools.partial(_combine_ln_kernel, R=R, alpha=alpha, nsteps=M // R),
        grid_spec=pltpu.PrefetchScalarGridSpec(
            num_scalar_prefetch=2, grid=(M // R,),
            in_specs=[pl.BlockSpec(memory_space=pl.ANY), pl.BlockSpec((R, TOP_K), row), pl.BlockSpec((R, D), row),
                      pl.BlockSpec((1, D), fixed), pl.BlockSpec((1, D), fixed)],
            out_specs=pl.BlockSpec((R, D), row),
            scratch_shapes=[pltpu.VMEM((2, TOP_K, R, D), F32), pltpu.SemaphoreType.DMA((2,))]),
        out_shape=jax.ShapeDtypeStruct((M, D), F32),
        compiler_params=_params("arbitrary"),
        name="moe_combine_ln",
    )(p0, p1, rows, w, h, g.reshape(1, D), b.reshape(1, D))


def _routing_tables(top_i, n_experts, tm):
    nt = top_i.shape[0]
    na = TOP_K * nt
    e = top_i.reshape(na)
    onehot = (e[None, :] == jnp.arange(n_experts, dtype=jnp.int32)[:, None]).astype(jnp.int32)
    csum = jnp.cumsum(onehot, axis=1)
    counts = csum[:, -1]
    padded = ((counts + tm - 1) // tm) * tm
    pend = jnp.cumsum(padded)
    pstart = pend - padded
    slot = jnp.sum(onehot * (csum - 1 + pstart[:, None]), axis=0)
    ntiles = (na + n_experts * (tm - 1) + tm - 1) // tm
    src = jnp.zeros(((ntiles + 1) * tm,), jnp.int32).at[slot].set(
        jnp.arange(na, dtype=jnp.int32) // TOP_K, unique_indices=True, indices_are_sorted=False)
    tile_start = jnp.arange(ntiles, dtype=jnp.int32) * tm
    used = (tile_start < pend[-1]).astype(jnp.int32)
    te = jnp.minimum(jnp.sum((tile_start[:, None] >= pend[None, :]).astype(jnp.int32), axis=1), n_experts - 1)
    last_used = jnp.max(jnp.where(used > 0, te, 0))
    te = jnp.where(used > 0, te, last_used)
    slot2 = slot.reshape(nt, TOP_K)
    return src, te, used, slot2[:, 0], slot2[:, 1]


def _t5_bucket_ids(dist, max_dist):
    max_exact = N_BUCKETS // 2
    d = np.asarray(dist, dtype=np.int64)
    log_ratio = np.log(np.maximum(d, max_exact) / max_exact) / math.log(max_dist / max_exact)
    large = np.minimum(max_exact + (log_ratio * (N_BUCKETS - max_exact)).astype(np.int64), N_BUCKETS - 1)
    return np.where(d < max_exact, d, large).astype(np.int32)


def _group_biases(rel_bias, H):
    max_dist = max(w for w, _ in DSW_GROUPS)
    out = []
    for g, (w, d) in enumerate(DSW_GROUPS):
        buckets = _t5_bucket_ids(d * np.arange(w // d + 1), max_dist)
        out.append(rel_bias[buckets][:, g * H:(g + 1) * H].astype(F32))
    return out


def _prompt_bias_vecs(biases):
    BQ = ATTN_BLOCK
    vecs = []
    for bg in biases:
        v = jnp.concatenate([bg[::-1], jnp.full((2 * BQ - 1, bg.shape[1]), NEG_INF, F32)], axis=0)
        vecs.append(jnp.transpose(v)[:, None, :])
    return jnp.stack(vecs)


def _sample_bias_rows(biases):
    BQ = ATTN_BLOCK
    rows = []
    for bg in biases:
        rows.append(jnp.concatenate([bg[::-1][:BQ], bg[0:1], jnp.zeros((7, bg.shape[1]), F32)], axis=0))
    return jnp.stack(rows)[..., None]


def kernel(x_prompt, x_sample, state_gla, cache_k, cache_v, ln_g, ln_b, gla_w_in, gla_w_a2, gla_b_a,
           gla_norm_g, gla_w_out, kv_w, dsw_w_q, dsw_w_out, rel_bias, ffn_w_gu, ffn_w_down,
           moe_w_router, moe_w_gu, moe_w_down):
    B, T, D = x_prompt.shape
    Bd = x_sample.shape[0]
    depth = ln_g.shape[0]
    assert depth == 2 and x_sample.shape[1] == 1
    alpha = (2 * depth) ** 0.25
    rank, dkt = gla_w_a2.shape[1:]
    dv = gla_norm_g.shape[1]
    dvt = gla_w_out.shape[1]
    Hg = dvt // dv
    dk = dkt // Hg
    G = len(DSW_GROUPS)
    Ha = rel_bias.shape[1] // G
    hd = D // Ha
    n_experts = moe_w_router.shape[2]
    f_dense = ffn_w_down.shape[1]
    f_exp = moe_w_down.shape[2]
    N = B * T

    w_in = gla_w_in[0]
    c_a = 2 * dkt + dvt
    w_in_all = jnp.concatenate([w_in[:, :c_a], w_in[:, c_a + rank:],
                                jnp.pad(w_in[:, c_a:c_a + rank], ((0, 0), (0, LANES - rank)))], axis=1).astype(BF16)
    n_in = w_in_all.shape[1]
    tn_in = max(t for t in range(LANES, 1024 + 1, LANES) if n_in % t == 0)
    w_a2p = jnp.pad(gla_w_a2[0], ((0, LANES - rank), (0, 0)))
    w_a2tp = jnp.pad(gla_w_a2[0].T, ((0, 0), (0, LANES - rank)))
    w_gout = gla_w_out[0].astype(BF16)
    w_k = kv_w[:, :D].astype(BF16)
    w_v = kv_w[:, D:].astype(BF16)
    w_q = dsw_w_q[0].astype(BF16)
    w_aout = dsw_w_out[0].astype(BF16)
    w_router = jnp.pad(moe_w_router[0], ((0, 0), (0, LANES - n_experts)))
    w_router_hi = w_router.astype(BF16)
    w_router_p = jnp.concatenate([w_router_hi, (w_router - w_router_hi.astype(F32)).astype(BF16)], axis=1)
    biases = _group_biases(rel_bias, Ha)
    tf_dense = math.gcd(f_dense, 512)
    tm_moe = 512
    tf_exp = max(t for t in range(256, 1024 + 1, 256) if f_exp % t == 0)

    def layer0_in(x2):
        return _mm(x2, w_in_all, tn=tn_in, name="gla_in_proj")

    def layer0_out(o_gla, x2, side=None):
        m = x2.shape[0]
        h1, h1b = _mm_res_ln(o_gla, w_gout, x2, ln_g[0, 0], ln_b[0, 0], alpha, name="gla_out_ln")
        h2, h2b, *side_out = _ffn(h1b, w_ffn_gu, w_ffn_down, h1, ln_g[0, 1], ln_b[0, 1], alpha,
                                  tm=min(512, m), tf=tf_dense, side=side, name="ffn_dense")
        k = _mm(h2b, w_k, name="k_proj")
        v = _mm(h2b, w_v, name="v_proj")
        q = _mm(h2b, w_q, scale=hd ** -0.5, name="q_proj")
        return (h2, k, v, q, *side_out)

    def attn_out(o_att, h2, **kw):
        return _mm_res_ln(o_att, w_aout, h2, ln_g[1, 0], ln_b[1, 0], alpha, want_bf16=False,
                          router_w=w_router_p, n_experts=n_experts, name="attn_out_ln", **kw)

    xp = x_prompt.reshape(N, D)
    proj_p = layer0_in(xp)
    o_gla_p, state_p, w_ffn_gu, w_ffn_down = _gla_prompt(
        proj_p, w_a2p, gla_b_a[0], gla_norm_g[0], sides=(ffn_w_gu[0], ffn_w_down[0]), B=B, T=T, H=Hg, dk=dk, dv=dv)
    h2_p, k_p, v_p, q_p, w_moe_down = layer0_out(o_gla_p, xp, side=moe_w_down[0].reshape(n_experts * f_exp, D))
    w_moe_down = w_moe_down.reshape(n_experts, f_exp, D)
    o_att_p, w_moe_gu = _attn_prompt(q_p.reshape(B, T, G * D), k_p.reshape(B, T, D), v_p.reshape(B, T, D),
                                     _prompt_bias_vecs(biases), moe_w_gu[0].reshape(n_experts * D, 2 * f_exp),
                                     B=B, T=T, H=Ha, hd=hd)
    w_moe_gu = w_moe_gu.reshape(n_experts, D, 2 * f_exp)
    h3, route_p = attn_out(o_att_p.reshape(N, D), h2_p, extra_rows=Bd)

    xs = x_sample.reshape(Bd, D)
    proj_s = layer0_in(xs)
    o_gla_s, state_s = _gla_sample(proj_s[:, :n_in - LANES], proj_s[:, n_in - LANES:], w_a2tp, gla_b_a[0],
                                   gla_norm_g[0], state_gla[0], H=Hg, dk=dk, dv=dv)
    h2_s, k_s, v_s, q_s = layer0_out(o_gla_s.reshape(Bd, dvt), xs)
    o_att_s = _attn_sample(q_s, k_s, v_s, cache_k, cache_v, _sample_bias_rows(biases), H=Ha, hd=hd)
    h3, route_s = attn_out(o_att_s.reshape(Bd, D), h2_s, into=h3)

    route = jnp.concatenate([route_p, route_s], axis=0)
    top_i = route[:, 0:TOP_K].astype(jnp.int32)
    top_w = route[:, TOP_K:2 * TOP_K]
    src, te, used, p0, p1 = _routing_tables(top_i, n_experts, tm_moe)
    y_sorted = _moe_ffn(h3, src, w_moe_gu, w_moe_down, te, used, tm=tm_moe, tf=tf_exp)
    y_p = _combine_ln(y_sorted, p0[:N], p1[:N], top_w[:N], h3, ln_g[1, 1], ln_b[1, 1], alpha)
    y_s = _combine_ln(y_sorted, p0[N:], p1[N:], top_w[N:], h3[N:], ln_g[1, 1], ln_b[1, 1], alpha)

    return (y_p.reshape(B, T, D), y_s.reshape(Bd, 1, D),
            state_p[None], state_s[None],
            k_p.reshape(B, T, Ha, hd), v_p.reshape(B, T, Ha, hd),
            k_s.reshape(Bd, 1, Ha, hd), v_s.reshape(Bd, 1, Ha, hd))
```
